```python
import jax, jax.numpy as jnp
from jax import lax
import numpy as np

D_MODEL = 2048
BATCH = 2
SEQ = 4096
DEPTH = 2
DEC_BATCH = 128
DEC_SEQ = 1
PAST_LEN = 16384
PAGE_SIZE = 128

N_MIXERS = 2
N_A_LAYERS = (DEPTH + 1) // 2
N_B_LAYERS = DEPTH // 2

A_HEADS = 16
A_Q_LORA = 512
A_KV_LORA = 512
A_NOPE = 128
A_ROPE = 64
A_VDIM = 128
A_QK = A_NOPE + A_ROPE
A_WIDTH = A_HEADS * A_VDIM
A_IN = A_Q_LORA + A_KV_LORA + A_ROPE + A_WIDTH
A_CACHE_W = A_KV_LORA + A_ROPE
A_Q_BLOCK = 128
ROPE_THETA = 10000.0

B_HEADS = 32
B_KV_HEADS = 4
B_GROUP = B_HEADS // B_KV_HEADS
B_HEAD_DIM = 64
B_WIDTH = B_HEADS * B_HEAD_DIM
B_KV_W = B_KV_HEADS * B_HEAD_DIM
B_IN = 2 * B_WIDTH + 2 * B_KV_W
WINDOW = 128

NORM_EPS = 1e-6
NEG_INF = -1e30
ADA_SCALE = 0.5

kernel_name = 'hybrid_mla_swa_adaln_step'


def _rmsnorm(x):
    xf = x.astype(jnp.float32)
    return (xf * lax.rsqrt(jnp.mean(xf * xf, -1, keepdims=True) + NORM_EPS)).astype(x.dtype)


def _modulate(x, c, w_ada, b_ada):
    mod = jax.nn.silu(c) @ w_ada + b_ada
    shift, scale, gate = jnp.split(mod, 3, -1)
    h = _rmsnorm(x) * (1 + scale[:, None, :]) + shift[:, None, :]
    return h, gate[:, None, :]


def _rope_tables(pos):
    inv = jnp.power(ROPE_THETA, -jnp.arange(0, A_ROPE, 2, dtype=jnp.float32) / A_ROPE)
    ang = pos.astype(jnp.float32)[:, None] * inv[None, :]
    return jnp.cos(ang), jnp.sin(ang)


def _rope(x, cos, sin):
    half = x.shape[-1] // 2
    x1, x2 = x[..., :half], x[..., half:]
    cos = cos.astype(x.dtype)
    sin = sin.astype(x.dtype)
    return jnp.concatenate([x1 * cos - x2 * sin, x1 * sin + x2 * cos], -1)


def _mla_project(h, pos, w_in, g_q, g_kv, w_uq):
    b, s, _ = h.shape
    proj = h @ w_in
    q_lat, kv_lat, k_pe, z = jnp.split(
        proj, [A_Q_LORA, A_Q_LORA + A_KV_LORA, A_Q_LORA + A_KV_LORA + A_ROPE], -1)
    q = ((_rmsnorm(q_lat) * g_q) @ w_uq).reshape(b, s, A_HEADS, A_QK)
    cos, sin = _rope_tables(pos)
    q_nope = q[..., :A_NOPE]
    q_pe = _rope(q[..., A_NOPE:], cos[:, None, :], sin[:, None, :])
    kv = _rmsnorm(kv_lat) * g_kv
    k_pe = _rope(k_pe, cos, sin)
    return q_nope, q_pe, kv, k_pe, z


def _mla_prompt(h, w_in, g_q, g_kv, w_uq, w_uk, w_uv, w_o):
    b, s, _ = h.shape
    q_nope, q_pe, kv, k_pe, z = _mla_project(h, jnp.arange(s), w_in, g_q, g_kv, w_uq)
    k_nope = jnp.einsum('bsc,chn->bshn', kv, w_uk)
    v = jnp.einsum('bsc,chv->bshv', kv, w_uv)
    q = jnp.concatenate([q_nope, q_pe], -1)
    k = jnp.concatenate(
        [k_nope, jnp.broadcast_to(k_pe[:, :, None, :], (b, s, A_HEADS, A_ROPE))], -1)
    n_blk = s // A_Q_BLOCK
    q_blocks = q.reshape(b, n_blk, A_Q_BLOCK, A_HEADS, A_QK).transpose(1, 0, 2, 3, 4)
    k_pos = jnp.arange(s)
    scale = A_QK ** -0.5

    def block(args):
        qb, start = args
        sc = jnp.einsum('bqhd,bkhd->bhqk', qb, k).astype(jnp.float32) * scale
        q_pos = start + jnp.arange(A_Q_BLOCK)
        sc = jnp.where(k_pos[None, :] <= q_pos[:, None], sc, NEG_INF)
        p = jax.nn.softmax(sc, -1).astype(v.dtype)
        return jnp.einsum('bhqk,bkhv->bqhv', p, v)

    o = lax.map(block, (q_blocks, jnp.arange(n_blk) * A_Q_BLOCK))
    o = o.transpose(1, 0, 2, 3, 4).reshape(b, s, A_WIDTH)
    y = (o * jax.nn.silu(z)) @ w_o
    return y, jnp.concatenate([kv, k_pe], -1)


def _mla_sample(h, cache_mla, a_idx, page_table, w_in, g_q, g_kv, w_uq, w_uk, w_uv, w_o):
    b, s, _ = h.shape
    f32 = jnp.float32
    q_nope, q_pe, kv, k_pe, z = _mla_project(h, PAST_LEN + jnp.arange(s), w_in, g_q, g_kv, w_uq)
    scale = A_QK ** -0.5
    q_abs = jnp.einsum('bshn,chn->bshc', q_nope, w_uk).astype(f32)
    q_pe32 = q_pe.astype(f32)

    def page_step(carry, pages):
        m, l, acc = carry
        rows = cache_mla[a_idx, pages].astype(f32)
        kv_r, pe_r = rows[..., :A_KV_LORA], rows[..., A_KV_LORA:]
        sc = (jnp.einsum('bshc,bpc->bshp', q_abs, kv_r)
              + jnp.einsum('bshr,bpr->bshp', q_pe32, pe_r)) * scale
        m_new = jnp.maximum(m, sc.max(-1))
        corr = jnp.exp(m - m_new)
        p = jnp.exp(sc - m_new[..., None])
        l = l * corr + p.sum(-1)
        acc = acc * corr[..., None] + jnp.einsum('bshp,bpc->bshc', p, kv_r)
        return (m_new, l, acc), None

    init = (jnp.full((b, s, A_HEADS), -jnp.inf, f32),
            jnp.zeros((b, s, A_HEADS), f32),
            jnp.zeros((b, s, A_HEADS, A_KV_LORA), f32))
    (m, l, acc), _ = lax.scan(page_step, init, page_table.T)

    kv32, pe32 = kv.astype(f32), k_pe.astype(f32)
    sc = (jnp.einsum('bshc,bjc->bshj', q_abs, kv32)
          + jnp.einsum('bshr,bjr->bshj', q_pe32, pe32)) * scale
    causal = jnp.arange(s)[None, :] <= jnp.arange(s)[:, None]
    sc = jnp.where(causal[None, :, None, :], sc, NEG_INF)
    m2 = jnp.maximum(m, sc.max(-1))
    corr = jnp.exp(m - m2)
    p = jnp.exp(sc - m2[..., None])
    l = l * corr + p.sum(-1)
    acc = acc * corr[..., None] + jnp.einsum('bshj,bjc->bshc', p, kv32)
    o_lat = (acc / l[..., None]).astype(h.dtype)
    o = jnp.einsum('bshc,chv->bshv', o_lat, w_uv).reshape(b, s, A_WIDTH)
    y = (o * jax.nn.silu(z)) @ w_o
    return y, jnp.concatenate([kv, k_pe], -1)


def _swa_project(h, w_in, b_in):
    b, s, _ = h.shape
    proj = h @ w_in + b_in
    q, k, v, z = jnp.split(proj, [B_WIDTH, B_WIDTH + B_KV_W, B_WIDTH + 2 * B_KV_W], -1)
    q = q.reshape(b, s, B_KV_HEADS, B_GROUP, B_HEAD_DIM)
    k = k.reshape(b, s, B_KV_HEADS, B_HEAD_DIM)
    v = v.reshape(b, s, B_KV_HEADS, B_HEAD_DIM)
    return q, k, v, z


def _alibi_slopes():
    slopes = jnp.power(2.0, -8.0 * jnp.arange(1, B_HEADS + 1, dtype=jnp.float32) / B_HEADS)
    return slopes.reshape(B_KV_HEADS, B_GROUP)


def _sink_softmax(sc, sinks):
    sk = sinks.astype(jnp.float32)[:, :, None, None]
    m = jnp.maximum(sc.max(-1, keepdims=True), sk)
    p = jnp.exp(sc - m)
    return p / (p.sum(-1, keepdims=True) + jnp.exp(sk - m))


def _swa_prompt(h, w_in, b_in, sinks, w_o, b_o):
    b, s, _ = h.shape
    q, k, v, z = _swa_project(h, w_in, b_in)
    nb = s // WINDOW
    qb = q.reshape(b, nb, WINDOW, B_KV_HEADS, B_GROUP, B_HEAD_DIM)

    def band(t):
        t = t.reshape(b, nb, WINDOW, B_KV_HEADS, B_HEAD_DIM)
        prev = jnp.concatenate([jnp.zeros_like(t[:, :1]), t[:, :-1]], 1)
        return jnp.concatenate([prev, t], 2)

    kband, vband = band(k), band(v)
    i = jnp.arange(WINDOW)[:, None]
    j = jnp.arange(2 * WINDOW)[None, :]
    dist = i - j + WINDOW
    valid = (dist >= 0) & (dist < WINDOW)
    blk = jnp.arange(nb)[:, None, None]
    valid = valid[None] & ((blk > 0) | (j[None] >= WINDOW))
    bias = -_alibi_slopes()[:, :, None, None] * dist.astype(jnp.float32)
    sc = jnp.einsum('bnqkgd,bnjkd->bnkgqj', qb, kband).astype(jnp.float32) * B_HEAD_DIM ** -0.5 + bias
    sc = jnp.where(valid[None, :, None, None], sc, NEG_INF)
    p = _sink_softmax(sc, sinks.reshape(B_KV_HEADS, B_GROUP)).astype(v.dtype)
    o = jnp.einsum('bnkgqj,bnjkd->bnqkgd', p, vband).reshape(b, s, B_WIDTH)
    y = (o * jax.nn.silu(z)) @ w_o + b_o
    wp = min(WINDOW, s)
    return y, k[:, s - wp:], v[:, s - wp:]


def _swa_sample(h, buf_k, buf_v, w_in, b_in, sinks, w_o, b_o):
    b, s, _ = h.shape
    q, k, v, z = _swa_project(h, w_in, b_in)
    wb = buf_k.shape[1]
    k_all = jnp.concatenate([buf_k.astype(k.dtype), k], 1)
    v_all = jnp.concatenate([buf_v.astype(v.dtype), v], 1)
    q_pos = PAST_LEN + jnp.arange(s)
    k_pos = PAST_LEN - wb + jnp.arange(wb + s)
    dist = q_pos[:, None] - k_pos[None, :]
    valid = (dist >= 0) & (dist < WINDOW)
    bias = -_alibi_slopes()[:, :, None, None] * dist.astype(jnp.float32)
    sc = jnp.einsum('bqkgd,bjkd->bkgqj', q, k_all).astype(jnp.float32) * B_HEAD_DIM ** -0.5 + bias
    sc = jnp.where(valid, sc, NEG_INF)
    p = _sink_softmax(sc, sinks.reshape(B_KV_HEADS, B_GROUP)).astype(v.dtype)
    o = jnp.einsum('bkgqj,bjkd->bqkgd', p, v_all).reshape(b, s, B_WIDTH)
    y = (o * jax.nn.silu(z)) @ w_o + b_o
    return y, k_all[:, -wb:], v_all[:, -wb:]


def setup_inputs(seed: int = 0) -> dict:
    key = jax.random.key(seed)
    ks = jax.random.split(key, 24)
    f32 = jnp.float32

    def nrm(k, shape, fan_in):
        return jax.random.normal(k, shape, f32) * fan_in ** -0.5

    n_pages = PAST_LEN // PAGE_SIZE
    n_used = DEC_BATCH * n_pages
    n_pool = n_used + (n_used + 3) // 4
    page_table = jax.random.permutation(ks[5], n_pool)[:n_used].reshape(DEC_BATCH, n_pages).astype(jnp.int32)
    win_buf = min(WINDOW, PAST_LEN)
    return {
        'x_prompt': jax.random.normal(ks[0], (BATCH, SEQ, D_MODEL), f32),
        'x_sample': jax.random.normal(ks[1], (DEC_BATCH, DEC_SEQ, D_MODEL), f32),
        'cache_mla': jax.random.normal(ks[2], (N_A_LAYERS, n_pool, PAGE_SIZE, A_CACHE_W), f32),
        'state_swa_k': jax.random.normal(ks[3], (N_B_LAYERS, DEC_BATCH, win_buf, B_KV_HEADS, B_HEAD_DIM), f32),
        'state_swa_v': jax.random.normal(ks[4], (N_B_LAYERS, DEC_BATCH, win_buf, B_KV_HEADS, B_HEAD_DIM), f32),
        'page_table': page_table,
        'c_prompt': jax.random.normal(ks[6], (BATCH, D_MODEL), f32),
        'c_sample': jax.random.normal(ks[7], (DEC_BATCH, D_MODEL), f32),
        'w_ada': nrm(ks[8], (DEPTH, D_MODEL, 3 * D_MODEL), D_MODEL) * ADA_SCALE,
        'b_ada': 0.02 * jax.random.normal(ks[9], (DEPTH, 3 * D_MODEL), f32),
        'wa_in': nrm(ks[10], (N_A_LAYERS, D_MODEL, A_IN), D_MODEL),
        'ga_q': 1.0 + 0.02 * jax.random.normal(ks[11], (N_A_LAYERS, A_Q_LORA), f32),
        'ga_kv': 1.0 + 0.02 * jax.random.normal(ks[12], (N_A_LAYERS, A_KV_LORA), f32),
        'wa_uq': nrm(ks[13], (N_A_LAYERS, A_Q_LORA, A_HEADS * A_QK), A_Q_LORA),
        'wa_uk': nrm(ks[14], (N_A_LAYERS, A_KV_LORA, A_HEADS, A_NOPE), A_KV_LORA),
        'wa_uv': nrm(ks[15], (N_A_LAYERS, A_KV_LORA, A_HEADS, A_VDIM), A_KV_LORA),
        'wa_o': nrm(ks[16], (N_A_LAYERS, A_WIDTH, D_MODEL), A_WIDTH),
        'wb_in': nrm(ks[17], (N_B_LAYERS, D_MODEL, B_IN), D_MODEL),
        'bb_in': 0.02 * jax.random.normal(ks[18], (N_B_LAYERS, B_IN), f32),
        'wb_sinks': jax.random.normal(ks[19], (N_B_LAYERS, B_HEADS), f32),
        'wb_o': nrm(ks[20], (N_B_LAYERS, B_WIDTH, D_MODEL), B_WIDTH),
        'bb_o': 0.02 * jax.random.normal(ks[21], (N_B_LAYERS, D_MODEL), f32),
        'g_final': 1.0 + 0.02 * jax.random.normal(ks[22], (D_MODEL,), f32),
    }


def reference(x_prompt, x_sample, cache_mla, state_swa_k, state_swa_v, page_table, c_prompt, c_sample,
              w_ada, b_ada, wa_in, ga_q, ga_kv, wa_uq, wa_uk, wa_uv, wa_o,
              wb_in, bb_in, wb_sinks, wb_o, bb_o, g_final):
    xp, xs = x_prompt, x_sample
    mla_p, mla_s, swk_p, swv_p, swk_s, swv_s = [], [], [], [], [], []
    for i in range(DEPTH):
        j = i // N_MIXERS
        hp, gate_p = _modulate(xp, c_prompt, w_ada[i], b_ada[i])
        hs, gate_s = _modulate(xs, c_sample, w_ada[i], b_ada[i])
        if i % N_MIXERS == 0:
            yp, rows_p = _mla_prompt(hp, wa_in[j], ga_q[j], ga_kv[j], wa_uq[j], wa_uk[j], wa_uv[j], wa_o[j])
            ys, rows_s = _mla_sample(hs, cache_mla, j, page_table, wa_in[j], ga_q[j], ga_kv[j],
                                     wa_uq[j], wa_uk[j], wa_uv[j], wa_o[j])
            mla_p.append(rows_p)
            mla_s.append(rows_s)
        else:
            yp, kp, vp = _swa_prompt(hp, wb_in[j], bb_in[j], wb_sinks[j], wb_o[j], bb_o[j])
            ys, kss, vss = _swa_sample(hs, state_swa_k[j], state_swa_v[j], wb_in[j], bb_in[j],
                                       wb_sinks[j], wb_o[j], bb_o[j])
            swk_p.append(kp)
            swv_p.append(vp)
            swk_s.append(kss)
            swv_s.append(vss)
        xp = xp + gate_p * yp
        xs = xs + gate_s * ys
    y_prompt = _rmsnorm(xp) * g_final
    y_sample = _rmsnorm(xs) * g_final
    return (y_prompt, y_sample, jnp.stack(mla_p), jnp.stack(mla_s),
            jnp.stack(swk_p), jnp.stack(swv_p), jnp.stack(swk_s), jnp.stack(swv_s))
```

```python
import functools

import jax
import jax.numpy as jnp
from jax import lax
from jax.experimental import pallas as pl
from jax.experimental.pallas import tpu as pltpu

F32 = jnp.float32
BF16 = jnp.bfloat16

D_MODEL = 2048
BATCH = 2
SEQ = 4096
DEC_BATCH = 128
PAST_LEN = 16384
PAGE_SIZE = 128
N_PAGES = PAST_LEN // PAGE_SIZE

A_HEADS = 16
A_Q_LORA = 512
A_KV_LORA = 512
A_NOPE = 128
A_ROPE = 64
A_VDIM = 128
A_QK = A_NOPE + A_ROPE
A_WIDTH = A_HEADS * A_VDIM
A_CACHE_W = A_KV_LORA + A_ROPE
A_HEAD_PAD = 256
A_IN_EXT = A_Q_LORA + A_KV_LORA + 2 * A_ROPE + A_WIDTH
ROPE_THETA = 10000.0

B_HEADS = 32
B_KV_HEADS = 4
B_GROUP = 8
B_HEAD_DIM = 64
B_WIDTH = B_HEADS * B_HEAD_DIM
B_KV_W = B_KV_HEADS * B_HEAD_DIM
B_IN = 2 * B_WIDTH + 2 * B_KV_W
WINDOW = 128

NORM_EPS = 1e-6
NEG_INF = -1e30
A_SCALE = A_QK ** -0.5
B_SCALE = B_HEAD_DIM ** -0.5

LANE = 128
VMEM_LIMIT = 56 * 1024 * 1024

_NT = (((1,), (1,)), ((), ()))


def _params(n_grid):
    return pltpu.CompilerParams(
        dimension_semantics=("arbitrary",) * n_grid, vmem_limit_bytes=VMEM_LIMIT)


def _resident(shape):
    nd = len(shape)
    return pl.BlockSpec(shape, lambda *_: (0,) * nd, pipeline_mode=pl.Buffered(1))


def _rms(x):
    return x * lax.rsqrt(jnp.mean(x * x, -1, keepdims=True) + NORM_EPS)


def _silu(x):
    return x * jax.nn.sigmoid(x)


def _dot(a, b):
    return jnp.dot(a, b, preferred_element_type=F32)


def _dot_nt(a, b):
    return lax.dot_general(a, b, _NT, preferred_element_type=F32)


def _ada_kernel(c_ref, w_ref, b_ref, o_ref):
    a = _silu(c_ref[...]).astype(BF16)
    o_ref[...] = _dot(a, w_ref[...].astype(BF16)) + b_ref[...]


def _ada_mod(c_all, w_ada, b_ada):
    depth, _, n = w_ada.shape
    m = c_all.shape[0]
    tn = 1024
    return pl.pallas_call(
        _ada_kernel,
        out_shape=jax.ShapeDtypeStruct((depth, m, n), F32),
        grid=(depth, n // tn),
        in_specs=[
            pl.BlockSpec((m, D_MODEL), lambda i, j: (0, 0)),
            pl.BlockSpec((None, D_MODEL, tn), lambda i, j: (i, 0, j)),
            pl.BlockSpec((None, 1, tn), lambda i, j: (i, 0, j)),
        ],
        out_specs=pl.BlockSpec((None, m, tn), lambda i, j: (i, 0, j)),
        compiler_params=_params(2),
        name="ada_mod",
    )(c_all, w_ada, b_ada.reshape(depth, 1, n))


def _mod_specs(per_token, tm, tiles_per_seq):
    if per_token:
        return pl.BlockSpec((tm, D_MODEL), lambda i: (i, 0))
    return pl.BlockSpec((None, 1, D_MODEL), lambda i: (i // tiles_per_seq, 0, 0))


def _rope_slab(slab, cc, ss):
    return slab * cc + pltpu.roll(slab, A_ROPE, 1) * ss


def _mla_proj_kernel(decode, x_ref, shift_ref, scale_ref, cc_ref, ss_ref, win_ref, gq_ref,
                     gkv_ref, wuq_ref, wa_ref, wb_ref, *outs):
    if decode:
        z_ref, rows_ref, qabs_ref, qpe_ref = outs
    else:
        z_ref, rows_ref, qcat_ref, kcat_ref, v_ref = outs
    h = (_rms(x_ref[...]) * (1.0 + scale_ref[...]) + shift_ref[...]).astype(BF16)
    o_q, o_kv, o_pe, o_z = 0, A_Q_LORA, A_Q_LORA + A_KV_LORA, A_Q_LORA + A_KV_LORA + 2 * A_ROPE
    q_lat = _dot(h, win_ref[:, o_q:o_kv])
    kv_lat = _dot(h, win_ref[:, o_kv:o_pe])
    pe2 = _dot(h, win_ref[:, o_pe:o_z])
    z_ref[...] = _dot(h, win_ref[:, o_z:])
    cc = cc_ref[...]
    ss = ss_ref[...]
    kv = _rms(kv_lat) * gkv_ref[...]
    kpe = _rope_slab(pe2, cc, ss)
    rows_ref[:, :A_KV_LORA] = kv
    rows_ref[:, A_KV_LORA:] = kpe[:, :A_ROPE]
    qn = (_rms(q_lat) * gq_ref[...]).astype(BF16)
    if not decode:
        kvb = kv.astype(BF16)
        v_ref[...] = _dot(kvb, wb_ref[...]).astype(BF16)
        kpe_b = kpe.astype(BF16)
    for hd in range(A_HEADS):
        lo = hd * A_HEAD_PAD
        qh = _dot(qn, wuq_ref[:, lo:lo + A_HEAD_PAD])
        nope = qh[:, :A_NOPE]
        qpe = (_rope_slab(qh[:, A_NOPE:], cc, ss) * A_SCALE).astype(BF16)
        if decode:
            qabs = _dot(nope.astype(BF16), wa_ref[hd])
            qabs_ref[:, hd * A_KV_LORA:(hd + 1) * A_KV_LORA] = (qabs * A_SCALE).astype(BF16)
            qpe_ref[:, hd * LANE:(hd + 1) * LANE] = qpe
        else:
            qcat_ref[:, lo:lo + A_NOPE] = (nope * A_SCALE).astype(BF16)
            qcat_ref[:, lo + A_NOPE:lo + A_HEAD_PAD] = qpe
            knope = _dot(kvb, wa_ref[:, hd * A_NOPE:(hd + 1) * A_NOPE])
            kcat_ref[:, lo:lo + A_NOPE] = knope.astype(BF16)
            kcat_ref[:, lo + A_NOPE:lo + A_HEAD_PAD] = kpe_b


def _mla_proj(decode, x, shift, scale, cc, ss, win, gq, gkv, wuq, wa, wb, tm, tiles_per_seq):
    m = x.shape[0]
    mod_spec = _mod_specs(decode, tm, tiles_per_seq)
    tab_spec = pl.BlockSpec((tm, LANE), lambda i: (i % tiles_per_seq, 0))
    row = lambda n: pl.BlockSpec((tm, n), lambda i: (i, 0))
    out_shape = [jax.ShapeDtypeStruct((m, A_WIDTH), F32),
                 jax.ShapeDtypeStruct((m, A_CACHE_W), F32)]
    out_specs = [row(A_WIDTH), row(A_CACHE_W)]
    if decode:
        out_shape += [jax.ShapeDtypeStruct((m, A_HEADS * A_KV_LORA), BF16),
                      jax.ShapeDtypeStruct((m, A_HEADS * LANE), BF16)]
        out_specs += [row(A_HEADS * A_KV_LORA), row(A_HEADS * LANE)]
    else:
        out_shape += [jax.ShapeDtypeStruct((m, A_HEADS * A_HEAD_PAD), BF16),
                      jax.ShapeDtypeStruct((m, A_HEADS * A_HEAD_PAD), BF16),
                      jax.ShapeDtypeStruct((m, A_WIDTH), BF16)]
        out_specs += [row(A_HEADS * A_HEAD_PAD), row(A_HEADS * A_HEAD_PAD), row(A_WIDTH)]
    return pl.pallas_call(
        functools.partial(_mla_proj_kernel, decode),
        out_shape=out_shape,
        grid=(m // tm,),
        in_specs=[row(D_MODEL), mod_spec, mod_spec, tab_spec, tab_spec,
                  _resident(win.shape), _resident(gq.shape), _resident(gkv.shape),
                  _resident(wuq.shape), _resident(wa.shape), _resident(wb.shape)],
        out_specs=out_specs,
        compiler_params=_params(1),
        name="mla_proj_decode" if decode else "mla_proj_prompt",
    )(x, shift, scale, cc, ss, win, gq, gkv, wuq, wa, wb)


FLASH_BLOCK = 512


def _flash_kernel(q_ref, k_ref, v_ref, o_ref, m_ref, l_ref, acc_ref):
    t = FLASH_BLOCK
    qi = pl.program_id(2)
    q = q_ref[...]
    m_ref[...] = jnp.full(m_ref.shape, NEG_INF, F32)
    l_ref[...] = jnp.zeros(l_ref.shape, F32)
    acc_ref[...] = jnp.zeros(acc_ref.shape, F32)

    def update(ki, diagonal):
        start = pl.multiple_of(ki * t, t)
        s = _dot_nt(q, k_ref[pl.ds(start, t), :])
        if diagonal:
            row = lax.broadcasted_iota(jnp.int32, (t, t), 0)
            col = lax.broadcasted_iota(jnp.int32, (t, t), 1)
            s = jnp.where(col <= row, s, NEG_INF)
        m_prev = m_ref[...]
        m_new = jnp.maximum(m_prev, jnp.max(s, -1, keepdims=True))
        alpha = jnp.exp(m_prev - m_new)
        p = jnp.exp(s - m_new)
        l_ref[...] = alpha * l_ref[...] + jnp.sum(p, -1, keepdims=True)
        acc_ref[...] = alpha * acc_ref[...] + _dot(p.astype(BF16), v_ref[pl.ds(start, t), :])
        m_ref[...] = m_new

    def body(ki, carry):
        update(ki, False)
        return carry

    lax.fori_loop(0, qi, body, 0)
    update(qi, True)
    o_ref[...] = (acc_ref[...] / l_ref[...]).astype(o_ref.dtype)


def _flash(qcat, kcat, v):
    t = FLASH_BLOCK
    nq = SEQ // t
    return pl.pallas_call(
        _flash_kernel,
        out_shape=jax.ShapeDtypeStruct((BATCH * SEQ, A_WIDTH), BF16),
        grid=(BATCH, A_HEADS, nq),
        in_specs=[
            pl.BlockSpec((t, A_HEAD_PAD), lambda b, h, i: (b * nq + i, h)),
            pl.BlockSpec((SEQ, A_HEAD_PAD), lambda b, h, i: (b, h)),
            pl.BlockSpec((SEQ, A_VDIM), lambda b, h, i: (b, h)),
        ],
        out_specs=pl.BlockSpec((t, A_VDIM), lambda b, h, i: (b * nq + i, h)),
        scratch_shapes=[pltpu.VMEM((t, 1), F32), pltpu.VMEM((t, 1), F32),
                        pltpu.VMEM((t, A_VDIM), F32)],
        compiler_params=_params(3),
        name="mla_flash",
    )(qcat, kcat, v)


DEC_PAGES_PER_STEP = 16


def _mla_decode_kernel(pt_ref, qabs_ref, qpe_ref, row_ref, *rest):
    del pt_ref
    n = DEC_PAGES_PER_STEP
    page_refs = rest[:n]
    o_ref, m_ref, l_ref, acc_ref = rest[n:]
    j = pl.program_id(1)

    @pl.when(j == 0)
    def _():
        m_ref[...] = jnp.full(m_ref.shape, NEG_INF, F32)
        l_ref[...] = jnp.zeros(l_ref.shape, F32)
        acc_ref[...] = jnp.zeros(acc_ref.shape, F32)

    qa = qabs_ref[...]
    qp = qpe_ref[:, :A_ROPE]
    scores, kvs = [], []
    for r in page_refs:
        kvb = r[:, :A_KV_LORA].astype(BF16)
        peb = r[:, A_KV_LORA:].astype(BF16)
        scores.append(_dot_nt(qa, kvb) + _dot_nt(qp, peb))
        kvs.append(kvb)
    s = jnp.concatenate(scores, 1)
    m_prev = m_ref[...]
    m_new = jnp.maximum(m_prev, jnp.max(s, -1, keepdims=True))
    alpha = jnp.exp(m_prev - m_new)
    p = jnp.exp(s - m_new)
    l_ref[...] = alpha * l_ref[...] + jnp.sum(p, -1, keepdims=True)
    pb = p.astype(BF16)
    acc = alpha * acc_ref[...]
    for i, kvb in enumerate(kvs):
        acc = acc + _dot(pb[:, i * PAGE_SIZE:(i + 1) * PAGE_SIZE], kvb)
    acc_ref[...] = acc
    m_ref[...] = m_new

    @pl.when(j == pl.num_programs(1) - 1)
    def _():
        rb = row_ref[...].astype(BF16).astype(F32)
        s_new = (jnp.sum(qa.astype(F32) * rb[:, :A_KV_LORA], -1, keepdims=True)
                 + jnp.sum(qp.astype(F32) * rb[:, A_KV_LORA:], -1, keepdims=True))
        m_old = m_ref[...]
        m_fin = jnp.maximum(m_old, s_new)
        corr = jnp.exp(m_old - m_fin)
        p_new = jnp.exp(s_new - m_fin)
        l_fin = l_ref[...] * corr + p_new
        acc_fin = acc_ref[...] * corr + p_new.astype(BF16).astype(F32) * rb[:, :A_KV_LORA]
        o_ref[...] = acc_fin / l_fin


def _mla_decode(page_table, qabs, qpe, rows, cache):
    n = DEC_PAGES_PER_STEP

    def page_spec(i):
        return pl.BlockSpec((None, None, PAGE_SIZE, A_CACHE_W),
                            lambda b, j, pt: (0, pt[b, j * n + i], 0, 0))

    grid_spec = pltpu.PrefetchScalarGridSpec(
        num_scalar_prefetch=1,
        grid=(DEC_BATCH, N_PAGES // n),
        in_specs=[
            pl.BlockSpec((None, A_HEADS, A_KV_LORA), lambda b, j, pt: (b, 0, 0)),
            pl.BlockSpec((None, A_HEADS, LANE), lambda b, j, pt: (b, 0, 0)),
            pl.BlockSpec((None, 1, A_CACHE_W), lambda b, j, pt: (b, 0, 0)),
        ] + [page_spec(i) for i in range(n)],
        out_specs=pl.BlockSpec((None, A_HEADS, A_KV_LORA), lambda b, j, pt: (b, 0, 0)),
        scratch_shapes=[pltpu.VMEM((A_HEADS, 1), F32), pltpu.VMEM((A_HEADS, 1), F32),
                        pltpu.VMEM((A_HEADS, A_KV_LORA), F32)],
    )
    return pl.pallas_call(
        _mla_decode_kernel,
        out_shape=jax.ShapeDtypeStruct((DEC_BATCH, A_HEADS, A_KV_LORA), F32),
        grid_spec=grid_spec,
        compiler_params=_params(2),
        name="mla_decode",
    )(page_table, qabs, qpe, rows, *([cache] * n))


def _uv_kernel(o_ref, w_ref, out_ref):
    out_ref[...] = _dot(o_ref[...].astype(BF16), w_ref[...]).astype(out_ref.dtype)


def _mla_decode_uv(o_lat2d, wuv_heads):
    m = o_lat2d.shape[0]
    return pl.pallas_call(
        _uv_kernel,
        out_shape=jax.ShapeDtypeStruct((m, A_WIDTH), BF16),
        grid=(A_HEADS,),
        in_specs=[pl.BlockSpec((m, A_KV_LORA), lambda h: (0, h)),
                  pl.BlockSpec((None, A_KV_LORA, A_VDIM), lambda h: (h, 0, 0))],
        out_specs=pl.BlockSpec((m, A_VDIM), lambda h: (0, h)),
        compiler_params=_params(1),
        name="mla_decode_uv",
    )(o_lat2d, wuv_heads)


def _outproj_kernel(final, o_ref, z_ref, w_ref, b_ref, x_ref, gate_ref, *rest):
    g = (o_ref[...].astype(F32) * _silu(z_ref[...])).astype(BF16)
    y = _dot(g, w_ref[...]) + b_ref[...]
    x_new = x_ref[...] + gate_ref[...] * y
    if final:
        gfin_ref, out_ref = rest
        out_ref[...] = _rms(x_new) * gfin_ref[...]
    else:
        (out_ref,) = rest
        out_ref[...] = x_new


def _outproj(final, o, z, w, b, x, gate, gfin, per_token, tm, tiles_per_seq):
    m = x.shape[0]
    row = lambda n: pl.BlockSpec((tm, n), lambda i: (i, 0))
    in_specs = [row(o.shape[1]), row(z.shape[1]), _resident(w.shape), _resident(b.shape),
                row(D_MODEL), _mod_specs(per_token, tm, tiles_per_seq)]
    args = [o, z, w, b, x, gate]
    if final:
        in_specs.append(_resident(gfin.shape))
        args.append(gfin)
    return pl.pallas_call(
        functools.partial(_outproj_kernel, final),
        out_shape=jax.ShapeDtypeStruct((m, D_MODEL), F32),
        grid=(m // tm,),
        in_specs=in_specs,
        out_specs=row(D_MODEL),
        compiler_params=_params(1),
        name="outproj_final" if final else "outproj",
    )(*args)


def _swa_proj_kernel(x_ref, shift_ref, scale_ref, w_ref, b_ref, q_ref, k_ref, v_ref, z_ref):
    h = (_rms(x_ref[...]) * (1.0 + scale_ref[...]) + shift_ref[...]).astype(BF16)
    o_k, o_v, o_z = B_WIDTH, B_WIDTH + B_KV_W, B_WIDTH + 2 * B_KV_W
    q = _dot(h, w_ref[:, :o_k]) + b_ref[:, :o_k]
    q_ref[...] = (q * B_SCALE).astype(BF16)
    k_ref[...] = _dot(h, w_ref[:, o_k:o_v]) + b_ref[:, o_k:o_v]
    v_ref[...] = _dot(h, w_ref[:, o_v:o_z]) + b_ref[:, o_v:o_z]
    z_ref[...] = _dot(h, w_ref[:, o_z:]) + b_ref[:, o_z:]


def _swa_proj(x, shift, scale, w, b, per_token, tm, tiles_per_seq):
    m = x.shape[0]
    row = lambda n: pl.BlockSpec((tm, n), lambda i: (i, 0))
    mod_spec = _mod_specs(per_token, tm, tiles_per_seq)
    return pl.pallas_call(
        _swa_proj_kernel,
        out_shape=[jax.ShapeDtypeStruct((m, B_WIDTH), BF16),
                   jax.ShapeDtypeStruct((m, B_KV_W), F32),
                   jax.ShapeDtypeStruct((m, B_KV_W), F32),
                   jax.ShapeDtypeStruct((m, B_WIDTH), F32)],
        grid=(m // tm,),
        in_specs=[row(D_MODEL), mod_spec, mod_spec, _resident(w.shape), _resident(b.shape)],
        out_specs=[row(B_WIDTH), row(B_KV_W), row(B_KV_W), row(B_WIDTH)],
        compiler_params=_params(1),
        name="swa_proj",
    )(x, shift, scale, w, b)


def _alibi_slope(head):
    return 2.0 ** (-8.0 * (head + 1) / B_HEADS)


def _block_diag2(t):
    zero = jnp.zeros_like(t)
    return jnp.concatenate([jnp.concatenate([t, zero], 1), jnp.concatenate([zero, t], 1)], 0)


def _swa_prompt_kernel(sink_ref, q_ref, kc_ref, kp_ref, vc_ref, vp_ref, o_ref):
    nb = pl.program_id(1)
    w = WINDOW
    kband = jnp.concatenate([kp_ref[...], kc_ref[...]], 0).astype(BF16)
    vband = jnp.concatenate([vp_ref[...], vc_ref[...]], 0).astype(BF16)
    row = lax.broadcasted_iota(jnp.int32, (w, 2 * w), 0)
    col = lax.broadcasted_iota(jnp.int32, (w, 2 * w), 1)
    dist = row - col + w
    first = jnp.where(nb > 0, 0, w)
    valid = (dist >= 0) & (dist < w) & (col >= first)
    distf = dist.astype(F32)
    pair_w = 2 * B_HEAD_DIM
    for kvh in range(B_KV_HEADS):
        kb = _block_diag2(kband[:, kvh * B_HEAD_DIM:(kvh + 1) * B_HEAD_DIM])
        vb = _block_diag2(vband[:, kvh * B_HEAD_DIM:(kvh + 1) * B_HEAD_DIM])
        for gp in range(B_GROUP // 2):
            lo = (kvh * B_GROUP + 2 * gp) * B_HEAD_DIM
            s2 = _dot_nt(q_ref[:, lo:lo + pair_w], kb)
            probs = []
            for t in range(2):
                head = kvh * B_GROUP + 2 * gp + t
                s = s2[:, t * 2 * w:(t + 1) * 2 * w] - _alibi_slope(head) * distf
                s = jnp.where(valid, s, NEG_INF)
                sk = sink_ref[head]
                m = jnp.maximum(jnp.max(s, -1, keepdims=True), sk)
                p = jnp.exp(s - m)
                den = jnp.sum(p, -1, keepdims=True) + jnp.exp(sk - m)
                probs.append((p / den).astype(BF16))
            o2 = _dot(jnp.concatenate(probs, 1), vb)
            o_ref[:, lo:lo + pair_w] = o2.astype(o_ref.dtype)


def _swa_prompt(sinks, q, k, v):
    nb = SEQ // WINDOW
    cur = lambda b, n, s: (b * nb + n, 0)
    prev = lambda b, n, s: (b * nb + jnp.maximum(n - 1, 0), 0)
    grid_spec = pltpu.PrefetchScalarGridSpec(
        num_scalar_prefetch=1,
        grid=(BATCH, nb),
        in_specs=[pl.BlockSpec((WINDOW, B_WIDTH), cur),
                  pl.BlockSpec((WINDOW, B_KV_W), cur), pl.BlockSpec((WINDOW, B_KV_W), prev),
                  pl.BlockSpec((WINDOW, B_KV_W), cur), pl.BlockSpec((WINDOW, B_KV_W), prev)],
        out_specs=pl.BlockSpec((WINDOW, B_WIDTH), cur),
    )
    return pl.pallas_call(
        _swa_prompt_kernel,
        out_shape=jax.ShapeDtypeStruct((BATCH * SEQ, B_WIDTH), BF16),
        grid_spec=grid_spec,
        compiler_params=_params(2),
        name="swa_prompt",
    )(sinks, q, k, k, v, v)


SWA_DEC_TILE = 8


def _swa_decode_kernel(slope_ref, sink_ref, q_ref, knew_ref, vnew_ref, kbuf_ref, vbuf_ref,
                       o_ref, kout_ref, vout_ref):
    w = WINDOW
    last = lax.broadcasted_iota(jnp.int32, (w, B_KV_W), 0) == w - 1
    distf = (w - 1 - lax.broadcasted_iota(jnp.int32, (B_GROUP, w), 1)).astype(F32)
    for i in range(SWA_DEC_TILE):
        kb = jnp.where(last, knew_ref[i:i + 1, :], pltpu.roll(kbuf_ref[i], w - 1, 0))
        vb = jnp.where(last, vnew_ref[i:i + 1, :], pltpu.roll(vbuf_ref[i], w - 1, 0))
        kout_ref[i] = kb
        vout_ref[i] = vb
        kbb = kb.astype(BF16)
        vbb = vb.astype(BF16)
        for kvh in range(B_KV_HEADS):
            q8 = q_ref[i, kvh * B_GROUP:(kvh + 1) * B_GROUP, :].astype(BF16)
            lo = kvh * B_HEAD_DIM
            s = _dot_nt(q8, kbb[:, lo:lo + B_HEAD_DIM]) - slope_ref[kvh] * distf
            sk = sink_ref[kvh]
            m = jnp.maximum(jnp.max(s, -1, keepdims=True), sk)
            p = jnp.exp(s - m)
            den = jnp.sum(p, -1, keepdims=True) + jnp.exp(sk - m)
            o8 = _dot((p / den).astype(BF16), vbb[:, lo:lo + B_HEAD_DIM])
            o_ref[i, kvh * B_GROUP:(kvh + 1) * B_GROUP, :] = o8.astype(o_ref.dtype)


def _swa_decode(slopes, sinks, q, knew, vnew, kbuf, vbuf):
    t = SWA_DEC_TILE
    head_tab = pl.BlockSpec((B_KV_HEADS, B_GROUP, LANE), lambda i: (0, 0, 0))
    qo = pl.BlockSpec((t, B_HEADS, B_HEAD_DIM), lambda i: (i, 0, 0))
    new = pl.BlockSpec((t, B_KV_W), lambda i: (i, 0))
    buf = pl.BlockSpec((t, WINDOW, B_KV_W), lambda i: (i, 0, 0))
    return pl.pallas_call(
        _swa_decode_kernel,
        out_shape=[jax.ShapeDtypeStruct((DEC_BATCH, B_HEADS, B_HEAD_DIM), F32),
                   jax.ShapeDtypeStruct(kbuf.shape, F32),
                   jax.ShapeDtypeStruct(vbuf.shape, F32)],
        grid=(DEC_BATCH // t,),
        in_specs=[head_tab, head_tab, qo, new, new, buf, buf],
        out_specs=[qo, buf, buf],
        compiler_params=_params(1),
        name="swa_decode",
    )(slopes, sinks, q, knew, vnew, kbuf, vbuf)


def _rope_tables(pos):
    inv = jnp.power(ROPE_THETA, -jnp.arange(0, A_ROPE, 2, dtype=F32) / A_ROPE)
    ang = pos.astype(F32)[:, None] * inv[None, :]
    zero = jnp.zeros((pos.shape[0], LANE - A_ROPE), F32)
    cos, sin = jnp.cos(ang), jnp.sin(ang)
    return jnp.concatenate([cos, cos, zero], -1), jnp.concatenate([sin, sin, zero], -1)


def _with_rot(w):
    half = A_ROPE // 2
    return jnp.concatenate([w, -w[..., half:], w[..., :half]], -1)


def kernel(x_prompt, x_sample, cache_mla, state_swa_k, state_swa_v, page_table, c_prompt, c_sample,
           w_ada, b_ada, wa_in, ga_q, ga_kv, wa_uq, wa_uk, wa_uv, wa_o,
           wb_in, bb_in, wb_sinks, wb_o, bb_o, g_final):
    n_p = BATCH * SEQ
    xp = x_prompt.reshape(n_p, D_MODEL)
    xs = x_sample.reshape(DEC_BATCH, D_MODEL)

    pad = jnp.zeros((8 - BATCH, D_MODEL), F32)
    mod = _ada_mod(jnp.concatenate([c_sample, c_prompt, pad], 0), w_ada, b_ada)

    def mods(i):
        parts = jnp.split(mod[i], 3, -1)
        sample = [p[:DEC_BATCH] for p in parts]
        prompt = [p[DEC_BATCH:DEC_BATCH + BATCH].reshape(BATCH, 1, D_MODEL) for p in parts]
        return prompt, sample

    (shift_p, scale_p, gate_p), (shift_s, scale_s, gate_s) = mods(0)
    o_pe = A_Q_LORA + A_KV_LORA
    w_in = wa_in[0]
    win = jnp.concatenate([w_in[:, :o_pe], _with_rot(w_in[:, o_pe:o_pe + A_ROPE]),
                           w_in[:, o_pe + A_ROPE:]], -1).astype(BF16)
    wuq3 = wa_uq[0].reshape(A_Q_LORA, A_HEADS, A_QK)
    wuq = jnp.concatenate([wuq3[..., :A_NOPE], _with_rot(wuq3[..., A_NOPE:])], -1).reshape(
        A_Q_LORA, A_HEADS * A_HEAD_PAD).astype(BF16)
    wuk = wa_uk[0].reshape(A_KV_LORA, A_HEADS * A_NOPE).astype(BF16)
    wuv = wa_uv[0].reshape(A_KV_LORA, A_WIDTH).astype(BF16)
    wuk_t = wa_uk[0].transpose(1, 2, 0).astype(BF16)
    wuv_h = wa_uv[0].transpose(1, 0, 2).astype(BF16)
    gq = ga_q[0].reshape(1, A_Q_LORA)
    gkv = ga_kv[0].reshape(1, A_KV_LORA)
    wo_a = wa_o[0].astype(BF16)
    zero_bias = jnp.zeros((1, D_MODEL), F32)

    tm_p = 256
    tiles_p = SEQ // tm_p
    cc_p, ss_p = _rope_tables(jnp.arange(SEQ))
    z_p, rows_p, qcat, kcat, v_p = _mla_proj(
        False, xp, shift_p, scale_p, cc_p, ss_p, win, gq, gkv, wuq, wuk, wuv, tm_p, tiles_p)
    o_p = _flash(qcat, kcat, v_p)
    xp = _outproj(False, o_p, z_p, wo_a, zero_bias, xp, gate_p, None, False, tm_p, tiles_p)

    cc_s, ss_s = _rope_tables(jnp.full((DEC_BATCH,), PAST_LEN))
    z_s, rows_s, qabs, qpe = _mla_proj(
        True, xs, shift_s, scale_s, cc_s, ss_s, win, gq, gkv, wuq, wuk_t, wuv, DEC_BATCH, 1)
    o_lat = _mla_decode(page_table,
                        qabs.reshape(DEC_BATCH, A_HEADS, A_KV_LORA),
                        qpe.reshape(DEC_BATCH, A_HEADS, LANE),
                        rows_s.reshape(DEC_BATCH, 1, A_CACHE_W), cache_mla)
    o_s = _mla_decode_uv(o_lat.reshape(DEC_BATCH, A_HEADS * A_KV_LORA), wuv_h)
    xs = _outproj(False, o_s, z_s, wo_a, zero_bias, xs, gate_s, None, True, DEC_BATCH, 1)

    (shift_p, scale_p, gate_p), (shift_s, scale_s, gate_s) = mods(1)
    w_b = wb_in[0].astype(BF16)
    b_b = bb_in[0].reshape(1, B_IN)
    wo_b = wb_o[0].astype(BF16)
    bo_b = bb_o[0].reshape(1, D_MODEL)
    gfin = g_final.reshape(1, D_MODEL)
    sinks = wb_sinks[0]

    q_p, k_p, v_p, z_p = _swa_proj(xp, shift_p, scale_p, w_b, b_b, False, tm_p, tiles_p)
    o_p = _swa_prompt(sinks, q_p, k_p, v_p)
    y_p = _outproj(True, o_p, z_p, wo_b, bo_b, xp, gate_p, gfin, False, tm_p, tiles_p)

    q_s, k_s, v_s, z_s = _swa_proj(xs, shift_s, scale_s, w_b, b_b, True, DEC_BATCH, 1)
    heads = jnp.arange(B_HEADS, dtype=F32)
    slopes = jnp.power(2.0, -8.0 * (heads + 1.0) / B_HEADS)
    rep = lambda t: jnp.broadcast_to(t.reshape(B_KV_HEADS, B_GROUP, 1), (B_KV_HEADS, B_GROUP, LANE))
    o_s, kwin, vwin = _swa_decode(
        rep(slopes), rep(sinks), q_s.astype(F32).reshape(DEC_BATCH, B_HEADS, B_HEAD_DIM), k_s, v_s,
        state_swa_k[0].reshape(DEC_BATCH, WINDOW, B_KV_W),
        state_swa_v[0].reshape(DEC_BATCH, WINDOW, B_KV_W))
    y_s = _outproj(True, o_s.reshape(DEC_BATCH, B_WIDTH), z_s, wo_b, bo_b, xs, gate_s, gfin,
                   True, DEC_BATCH, 1)

    kv_shape = (1, BATCH, WINDOW, B_KV_HEADS, B_HEAD_DIM)
    win_shape = (1, DEC_BATCH, WINDOW, B_KV_HEADS, B_HEAD_DIM)
    return (y_p.reshape(BATCH, SEQ, D_MODEL),
            y_s.reshape(DEC_BATCH, 1, D_MODEL),
            rows_p.reshape(1, BATCH, SEQ, A_CACHE_W),
            rows_s.reshape(1, DEC_BATCH, 1, A_CACHE_W),
            k_p.reshape(BATCH, SEQ, B_KV_W)[:, SEQ - WINDOW:].reshape(kv_shape),
            v_p.reshape(BATCH, SEQ, B_KV_W)[:, SEQ - WINDOW:].reshape(kv_shape),
            kwin.reshape(win_shape),
            vwin.reshape(win_shape))
```

```python
import functools

import jax
import jax.numpy as jnp
from jax import lax
from jax.experimental import pallas as pl
from jax.experimental.pallas import tpu as pltpu

F32 = jnp.float32
BF16 = jnp.bfloat16

D_MODEL = 2048
BATCH = 2
SEQ = 4096
DEC_BATCH = 128
PAST_LEN = 16384
PAGE_SIZE = 128
N_PAGES = PAST_LEN // PAGE_SIZE

A_HEADS = 16
A_Q_LORA = 512
A_KV_LORA = 512
A_NOPE = 128
A_ROPE = 64
A_VDIM = 128
A_QK = A_NOPE + A_ROPE
A_WIDTH = A_HEADS * A_VDIM
A_CACHE_W = A_KV_LORA + A_ROPE
A_HEAD_PAD = 256
A_IN_EXT = A_Q_LORA + A_KV_LORA + 2 * A_ROPE + A_WIDTH
ROPE_THETA = 10000.0

B_HEADS = 32
B_KV_HEADS = 4
B_GROUP = 8
B_HEAD_DIM = 64
B_WIDTH = B_HEADS * B_HEAD_DIM
B_KV_W = B_KV_HEADS * B_HEAD_DIM
B_IN = 2 * B_WIDTH + 2 * B_KV_W
WINDOW = 128

NORM_EPS = 1e-6
NEG_INF = -1e30
A_SCALE = A_QK ** -0.5
B_SCALE = B_HEAD_DIM ** -0.5

LANE = 128
VMEM_LIMIT = 56 * 1024 * 1024

_NT = (((1,), (1,)), ((), ()))


def _params(n_grid):
    return pltpu.CompilerParams(
        dimension_semantics=("arbitrary",) * n_grid, vmem_limit_bytes=VMEM_LIMIT)


def _resident(shape):
    nd = len(shape)
    return pl.BlockSpec(shape, lambda *_: (0,) * nd, pipeline_mode=pl.Buffered(1))


def _rms(x):
    return x * lax.rsqrt(jnp.mean(x * x, -1, keepdims=True) + NORM_EPS)


def _silu(x):
    return x * jax.nn.sigmoid(x)


def _dot(a, b):
    return jnp.dot(a, b, preferred_element_type=F32)


def _dot_nt(a, b):
    return lax.dot_general(a, b, _NT, preferred_element_type=F32)


def _ada_kernel(c_ref, w_ref, b_ref, o_ref):
    a = _silu(c_ref[...]).astype(BF16)
    o_ref[...] = _dot(a, w_ref[...].astype(BF16)) + b_ref[...]


def _ada_mod(c_all, w_ada, b_ada):
    depth, _, n = w_ada.shape
    m = c_all.shape[0]
    tn = 1024
    return pl.pallas_call(
        _ada_kernel,
        out_shape=jax.ShapeDtypeStruct((depth, m, n), F32),
        grid=(depth, n // tn),
        in_specs=[
            pl.BlockSpec((m, D_MODEL), lambda i, j: (0, 0)),
            pl.BlockSpec((None, D_MODEL, tn), lambda i, j: (i, 0, j)),
            pl.BlockSpec((None, 1, tn), lambda i, j: (i, 0, j)),
        ],
        out_specs=pl.BlockSpec((None, m, tn), lambda i, j: (i, 0, j)),
        compiler_params=_params(2),
        name="ada_mod",
    )(c_all, w_ada, b_ada.reshape(depth, 1, n))


def _mod_specs(per_token, tm, tiles_per_seq):
    if per_token:
        return pl.BlockSpec((tm, D_MODEL), lambda i: (i, 0))
    return pl.BlockSpec((None, 1, D_MODEL), lambda i: (i // tiles_per_seq, 0, 0))


def _rope_slab(slab, cc, ss):
    return slab * cc + pltpu.roll(slab, A_ROPE, 1) * ss


def _mla_proj_kernel(decode, x_ref, shift_ref, scale_ref, cc_ref, ss_ref, win_ref, gq_ref,
                     gkv_ref, wuq_ref, wa_ref, wb_ref, *outs):
    if decode:
        z_ref, rows_ref, qabs_ref, qpe_ref = outs
    else:
        z_ref, rows_ref, qcat_ref, kcat_ref, v_ref = outs
    h = (_rms(x_ref[...]) * (1.0 + scale_ref[...]) + shift_ref[...]).astype(BF16)
    o_q, o_kv, o_pe, o_z = 0, A_Q_LORA, A_Q_LORA + A_KV_LORA, A_Q_LORA + A_KV_LORA + 2 * A_ROPE
    q_lat = _dot(h, win_ref[:, o_q:o_kv])
    kv_lat = _dot(h, win_ref[:, o_kv:o_pe])
    pe2 = _dot(h, win_ref[:, o_pe:o_z])
    z_ref[...] = _dot(h, win_ref[:, o_z:])
    cc = cc_ref[...]
    ss = ss_ref[...]
    kv = _rms(kv_lat) * gkv_ref[...]
    kpe = _rope_slab(pe2, cc, ss)
    rows_ref[:, :A_KV_LORA] = kv
    rows_ref[:, A_KV_LORA:] = kpe[:, :A_ROPE]
    qn = (_rms(q_lat) * gq_ref[...]).astype(BF16)
    if not decode:
        kvb = kv.astype(BF16)
        kpe_b = kpe.astype(BF16)
    for hd in range(A_HEADS):
        lo = hd * A_HEAD_PAD
        qh = _dot(qn, wuq_ref[:, lo:lo + A_HEAD_PAD])
        nope = qh[:, :A_NOPE]
        qpe = (_rope_slab(qh[:, A_NOPE:], cc, ss) * A_SCALE).astype(BF16)
        if decode:
            qabs = _dot(nope.astype(BF16), wa_ref[hd])
            qabs_ref[:, hd * A_KV_LORA:(hd + 1) * A_KV_LORA] = (qabs * A_SCALE).astype(BF16)
            qpe_ref[:, hd * LANE:(hd + 1) * LANE] = qpe
        else:
            qcat_ref[:, lo:lo + A_NOPE] = (nope * A_SCALE).astype(BF16)
            qcat_ref[:, lo + A_NOPE:lo + A_HEAD_PAD] = qpe
            knope = _dot(kvb, wa_ref[:, hd * A_NOPE:(hd + 1) * A_NOPE])
            kcat_ref[:, lo:lo + A_NOPE] = knope.astype(BF16)
            kcat_ref[:, lo + A_NOPE:lo + A_HEAD_PAD] = kpe_b
            wv_t = wb_ref[hd * A_VDIM:(hd + 1) * A_VDIM, :]
            v_ref[hd] = _dot_nt(wv_t, kvb).astype(BF16)


def _mla_proj(decode, x, shift, scale, cc, ss, win, gq, gkv, wuq, wa, wb, tm, tiles_per_seq):
    m = x.shape[0]
    mod_spec = _mod_specs(decode, tm, tiles_per_seq)
    tab_spec = pl.BlockSpec((tm, LANE), lambda i: (i % tiles_per_seq, 0))
    row = lambda n: pl.BlockSpec((tm, n), lambda i: (i, 0))
    out_shape = [jax.ShapeDtypeStruct((m, A_WIDTH), F32),
                 jax.ShapeDtypeStruct((m, A_CACHE_W), F32)]
    out_specs = [row(A_WIDTH), row(A_CACHE_W)]
    if decode:
        out_shape += [jax.ShapeDtypeStruct((m, A_HEADS * A_KV_LORA), BF16),
                      jax.ShapeDtypeStruct((m, A_HEADS * LANE), BF16)]
        out_specs += [row(A_HEADS * A_KV_LORA), row(A_HEADS * LANE)]
    else:
        out_shape += [jax.ShapeDtypeStruct((m, A_HEADS * A_HEAD_PAD), BF16),
                      jax.ShapeDtypeStruct((m, A_HEADS * A_HEAD_PAD), BF16),
                      jax.ShapeDtypeStruct(
                          (BATCH, A_HEADS, SEQ // FLASH_TK, A_VDIM, FLASH_TK), BF16)]
        per_blk = FLASH_TK // tm

        def v_map(i):
            t = i % tiles_per_seq
            return (i // tiles_per_seq, 0, t // per_blk, 0, t % per_blk)

        out_specs += [row(A_HEADS * A_HEAD_PAD), row(A_HEADS * A_HEAD_PAD),
                      pl.BlockSpec((None, A_HEADS, None, A_VDIM, tm), v_map)]
    return pl.pallas_call(
        functools.partial(_mla_proj_kernel, decode),
        out_shape=out_shape,
        grid=(m // tm,),
        in_specs=[row(D_MODEL), mod_spec, mod_spec, tab_spec, tab_spec,
                  _resident(win.shape), _resident(gq.shape), _resident(gkv.shape),
                  _resident(wuq.shape), _resident(wa.shape), _resident(wb.shape)],
        out_specs=out_specs,
        compiler_params=_params(1),
        name="mla_proj_decode" if decode else "mla_proj_prompt",
    )(x, shift, scale, cc, ss, win, gq, gkv, wuq, wa, wb)


FLASH_TQ = 1024
FLASH_TK = 512
FLASH_CHAIN = 256


def _flash_kernel(q_ref, k_ref, vt_ref, o_ref, m_ref, l_ref, acc_ref):
    tq, tk, cw = FLASH_TQ, FLASH_TK, FLASH_CHAIN
    n_chain = tq // cw
    qi = pl.program_id(2)
    m_ref[...] = jnp.full(m_ref.shape, NEG_INF, F32)
    l_ref[...] = jnp.zeros(l_ref.shape, F32)
    acc_ref[...] = jnp.zeros(acc_ref.shape, F32)

    def update(c, s, vt_blk):
        m_prev = m_ref[c]
        m_new = jnp.maximum(m_prev, jnp.max(s, 0, keepdims=True))
        alpha = jnp.exp(m_prev - m_new)
        p = jnp.exp(s - m_new)
        l_ref[c] = alpha * l_ref[c] + jnp.sum(p, 0, keepdims=True)
        acc_ref[c] = alpha * acc_ref[c] + _dot(vt_blk, p.astype(BF16))
        m_ref[c] = m_new

    def block(ki, key_off):
        start = pl.multiple_of(ki * tk, tk)
        k_blk = k_ref[pl.ds(start, tk), :]
        vt_blk = vt_ref[ki]
        chains = [c for c in range(n_chain)
                  if key_off is None or key_off <= (c + 1) * cw - 1]
        scores = [_dot_nt(k_blk, q_ref[c * cw:(c + 1) * cw, :]) for c in chains]
        for c, s in zip(chains, scores):
            if key_off is not None and key_off + tk - 1 > c * cw:
                key = lax.broadcasted_iota(jnp.int32, (tk, cw), 0) + key_off
                qry = lax.broadcasted_iota(jnp.int32, (tk, cw), 1) + c * cw
                s = jnp.where(key <= qry, s, NEG_INF)
            update(c, s, vt_blk)

    per_tile = tq // tk

    def body(ki, carry):
        block(ki, None)
        return carry

    lax.fori_loop(0, per_tile * qi, body, 0)
    for d in range(per_tile):
        block(per_tile * qi + d, d * tk)
    for c in range(n_chain):
        o_t = acc_ref[c] * (1.0 / l_ref[c])
        o_ref[c * cw:(c + 1) * cw, :] = o_t.T.astype(o_ref.dtype)


def _flash(qcat, kcat, vt):
    tq, tk, cw = FLASH_TQ, FLASH_TK, FLASH_CHAIN
    nq, nk, n_chain = SEQ // tq, SEQ // tk, tq // cw
    return pl.pallas_call(
        _flash_kernel,
        out_shape=jax.ShapeDtypeStruct((BATCH * SEQ, A_WIDTH), BF16),
        grid=(BATCH, A_HEADS, nq),
        in_specs=[
            pl.BlockSpec((tq, A_HEAD_PAD), lambda b, h, i: (b * nq + i, h)),
            pl.BlockSpec((SEQ, A_HEAD_PAD), lambda b, h, i: (b, h)),
            pl.BlockSpec((None, None, nk, A_VDIM, tk), lambda b, h, i: (b, h, 0, 0, 0)),
        ],
        out_specs=pl.BlockSpec((tq, A_VDIM), lambda b, h, i: (b * nq + i, h)),
        scratch_shapes=[pltpu.VMEM((n_chain, 1, cw), F32), pltpu.VMEM((n_chain, 1, cw), F32),
                        pltpu.VMEM((n_chain, A_VDIM, cw), F32)],
        compiler_params=_params(3),
        name="mla_flash",
    )(qcat, kcat, vt)


DEC_PAGES_PER_STEP = 32


def _mla_decode_kernel(pt_ref, qabs_ref, qpe_ref, row_ref, *rest):
    del pt_ref
    n = DEC_PAGES_PER_STEP
    page_refs = rest[:n]
    o_ref, m_ref, l_ref, acc_ref = rest[n:]
    j = pl.program_id(1)

    @pl.when(j == 0)
    def _():
        m_ref[...] = jnp.full(m_ref.shape, NEG_INF, F32)
        l_ref[...] = jnp.zeros(l_ref.shape, F32)
        acc_ref[...] = jnp.zeros(acc_ref.shape, F32)

    qa = qabs_ref[...]
    qp = qpe_ref[:, :A_ROPE]
    pages = [r[...].astype(BF16) for r in page_refs]
    kv_t = jnp.concatenate([pg[:A_KV_LORA] for pg in pages], 1)
    pe_t = jnp.concatenate([pg[A_KV_LORA:] for pg in pages], 1)
    s = _dot(qa, kv_t) + _dot(qp, pe_t)
    m_prev = m_ref[...]
    m_new = jnp.maximum(m_prev, jnp.max(s, -1, keepdims=True))
    alpha = jnp.exp(m_prev - m_new)
    p = jnp.exp(s - m_new)
    l_ref[...] = alpha * l_ref[...] + jnp.sum(p, -1, keepdims=True)
    acc_ref[...] = alpha * acc_ref[...] + _dot_nt(p.astype(BF16), kv_t)
    m_ref[...] = m_new

    @pl.when(j == pl.num_programs(1) - 1)
    def _():
        rb = row_ref[...].astype(BF16).astype(F32)
        s_new = (jnp.sum(qa.astype(F32) * rb[:, :A_KV_LORA], -1, keepdims=True)
                 + jnp.sum(qp.astype(F32) * rb[:, A_KV_LORA:], -1, keepdims=True))
        m_old = m_ref[...]
        m_fin = jnp.maximum(m_old, s_new)
        corr = jnp.exp(m_old - m_fin)
        p_new = jnp.exp(s_new - m_fin)
        l_fin = l_ref[...] * corr + p_new
        acc_fin = acc_ref[...] * corr + p_new.astype(BF16).astype(F32) * rb[:, :A_KV_LORA]
        o_ref[...] = acc_fin / l_fin


def _mla_decode(page_table, qabs, qpe, rows, cache):
    n = DEC_PAGES_PER_STEP

    def page_spec(i):
        return pl.BlockSpec((None, None, A_CACHE_W, PAGE_SIZE),
                            lambda b, j, pt: (0, pt[b, j * n + i], 0, 0))

    grid_spec = pltpu.PrefetchScalarGridSpec(
        num_scalar_prefetch=1,
        grid=(DEC_BATCH, N_PAGES // n),
        in_specs=[
            pl.BlockSpec((None, A_HEADS, A_KV_LORA), lambda b, j, pt: (b, 0, 0)),
            pl.BlockSpec((None, A_HEADS, LANE), lambda b, j, pt: (b, 0, 0)),
            pl.BlockSpec((None, 1, A_CACHE_W), lambda b, j, pt: (b, 0, 0)),
        ] + [page_spec(i) for i in range(n)],
        out_specs=pl.BlockSpec((None, A_HEADS, A_KV_LORA), lambda b, j, pt: (b, 0, 0)),
        scratch_shapes=[pltpu.VMEM((A_HEADS, 1), F32), pltpu.VMEM((A_HEADS, 1), F32),
                        pltpu.VMEM((A_HEADS, A_KV_LORA), F32)],
    )
    return pl.pallas_call(
        _mla_decode_kernel,
        out_shape=jax.ShapeDtypeStruct((DEC_BATCH, A_HEADS, A_KV_LORA), F32),
        grid_spec=grid_spec,
        compiler_params=_params(2),
        name="mla_decode",
    )(page_table, qabs, qpe, rows, *([cache] * n))


def _uv_kernel(o_ref, w_ref, out_ref):
    out_ref[...] = _dot(o_ref[...].astype(BF16), w_ref[...]).astype(out_ref.dtype)


def _mla_decode_uv(o_lat2d, wuv_heads):
    m = o_lat2d.shape[0]
    return pl.pallas_call(
        _uv_kernel,
        out_shape=jax.ShapeDtypeStruct((m, A_WIDTH), BF16),
        grid=(A_HEADS,),
        in_specs=[pl.BlockSpec((m, A_KV_LORA), lambda h: (0, h)),
                  pl.BlockSpec((None, A_KV_LORA, A_VDIM), lambda h: (h, 0, 0))],
        out_specs=pl.BlockSpec((m, A_VDIM), lambda h: (0, h)),
        compiler_params=_params(1),
        name="mla_decode_uv",
    )(o_lat2d, wuv_heads)


def _outproj_kernel(final, o_ref, z_ref, w_ref, b_ref, x_ref, gate_ref, *rest):
    g = (o_ref[...].astype(F32) * _silu(z_ref[...])).astype(BF16)
    y = _dot(g, w_ref[...]) + b_ref[...]
    x_new = x_ref[...] + gate_ref[...] * y
    if final:
        gfin_ref, out_ref = rest
        out_ref[...] = _rms(x_new) * gfin_ref[...]
    else:
        (out_ref,) = rest
        out_ref[...] = x_new


def _outproj(final, o, z, w, b, x, gate, gfin, per_token, tm, tiles_per_seq):
    m = x.shape[0]
    row = lambda n: pl.BlockSpec((tm, n), lambda i: (i, 0))
    in_specs = [row(o.shape[1]), row(z.shape[1]), _resident(w.shape), _resident(b.shape),
                row(D_MODEL), _mod_specs(per_token, tm, tiles_per_seq)]
    args = [o, z, w, b, x, gate]
    if final:
        in_specs.append(_resident(gfin.shape))
        args.append(gfin)
    return pl.pallas_call(
        functools.partial(_outproj_kernel, final),
        out_shape=jax.ShapeDtypeStruct((m, D_MODEL), F32),
        grid=(m // tm,),
        in_specs=in_specs,
        out_specs=row(D_MODEL),
        compiler_params=_params(1),
        name="outproj_final" if final else "outproj",
    )(*args)


def _swa_proj_kernel(x_ref, shift_ref, scale_ref, w_ref, b_ref, q_ref, k_ref, v_ref, z_ref):
    h = (_rms(x_ref[...]) * (1.0 + scale_ref[...]) + shift_ref[...]).astype(BF16)
    o_k, o_v, o_z = B_WIDTH, B_WIDTH + B_KV_W, B_WIDTH + 2 * B_KV_W
    q = _dot(h, w_ref[:, :o_k]) + b_ref[:, :o_k]
    q_ref[...] = (q * B_SCALE).astype(BF16)
    k_ref[...] = _dot(h, w_ref[:, o_k:o_v]) + b_ref[:, o_k:o_v]
    v_ref[...] = _dot(h, w_ref[:, o_v:o_z]) + b_ref[:, o_v:o_z]
    z_ref[...] = _dot(h, w_ref[:, o_z:]) + b_ref[:, o_z:]


def _swa_proj(x, shift, scale, w, b, per_token, tm, tiles_per_seq):
    m = x.shape[0]
    row = lambda n: pl.BlockSpec((tm, n), lambda i: (i, 0))
    mod_spec = _mod_specs(per_token, tm, tiles_per_seq)
    return pl.pallas_call(
        _swa_proj_kernel,
        out_shape=[jax.ShapeDtypeStruct((m, B_WIDTH), BF16),
                   jax.ShapeDtypeStruct((m, B_KV_W), F32),
                   jax.ShapeDtypeStruct((m, B_KV_W), F32),
                   jax.ShapeDtypeStruct((m, B_WIDTH), F32)],
        grid=(m // tm,),
        in_specs=[row(D_MODEL), mod_spec, mod_spec, _resident(w.shape), _resident(b.shape)],
        out_specs=[row(B_WIDTH), row(B_KV_W), row(B_KV_W), row(B_WIDTH)],
        compiler_params=_params(1),
        name="swa_proj",
    )(x, shift, scale, w, b)


def _block_diag2(t):
    zero = jnp.zeros_like(t)
    return jnp.concatenate([jnp.concatenate([t, zero], 1), jnp.concatenate([zero, t], 1)], 0)


def _swa_prompt_kernel(sink_ref, q_ref, kc_ref, kp_ref, vc_ref, vp_ref, bias_ref, o_ref):
    w = WINDOW
    kband = jnp.concatenate([kp_ref[...], kc_ref[...]], 0).astype(BF16)
    vband = jnp.concatenate([vp_ref[...], vc_ref[...]], 0).astype(BF16)
    pair_w = 2 * B_HEAD_DIM
    n_pair = B_HEADS // 2
    pairs_per_kv = B_GROUP // 2
    scores, vbs = [], []
    for kvh in range(B_KV_HEADS):
        kb = _block_diag2(kband[:, kvh * B_HEAD_DIM:(kvh + 1) * B_HEAD_DIM])
        vbs.append(_block_diag2(vband[:, kvh * B_HEAD_DIM:(kvh + 1) * B_HEAD_DIM]))
        for gp in range(pairs_per_kv):
            lo = (kvh * pairs_per_kv + gp) * pair_w
            scores.append(_dot_nt(q_ref[:, lo:lo + pair_w], kb))
    probs = []
    for idx in range(n_pair):
        both = []
        for t in range(2):
            head = 2 * idx + t
            s = scores[idx][:, t * 2 * w:(t + 1) * 2 * w] + bias_ref[head]
            sk = sink_ref[head]
            m = jnp.maximum(jnp.max(s, -1, keepdims=True), sk)
            p = jnp.exp(s - m)
            den = jnp.sum(p, -1, keepdims=True) + jnp.exp(sk - m)
            both.append((p * (1.0 / den)).astype(BF16))
        probs.append(jnp.concatenate(both, 1))
    for idx in range(n_pair):
        o2 = _dot(probs[idx], vbs[idx // pairs_per_kv])
        o_ref[:, idx * pair_w:(idx + 1) * pair_w] = o2.astype(o_ref.dtype)


def _swa_prompt_bias():
    i = jnp.arange(WINDOW)[:, None]
    j = jnp.arange(2 * WINDOW)[None, :]
    dist = i - j + WINDOW
    valid = (dist >= 0) & (dist < WINDOW)
    slopes = jnp.power(2.0, -8.0 * jnp.arange(1, B_HEADS + 1, dtype=F32) / B_HEADS)
    alibi = -slopes[:, None, None] * dist.astype(F32)
    later = jnp.where(valid[None], alibi, NEG_INF)
    first = jnp.where((valid & (j >= WINDOW))[None], alibi, NEG_INF)
    return jnp.stack([first, later])


def _swa_prompt(sinks, q, k, v):
    nb = SEQ // WINDOW
    cur = lambda b, n, s: (b * nb + n, 0)
    prev = lambda b, n, s: (b * nb + jnp.maximum(n - 1, 0), 0)
    grid_spec = pltpu.PrefetchScalarGridSpec(
        num_scalar_prefetch=1,
        grid=(BATCH, nb),
        in_specs=[pl.BlockSpec((WINDOW, B_WIDTH), cur),
                  pl.BlockSpec((WINDOW, B_KV_W), cur), pl.BlockSpec((WINDOW, B_KV_W), prev),
                  pl.BlockSpec((WINDOW, B_KV_W), cur), pl.BlockSpec((WINDOW, B_KV_W), prev),
                  pl.BlockSpec((None, B_HEADS, WINDOW, 2 * WINDOW),
                               lambda b, n, s: (jnp.minimum(n, 1), 0, 0, 0))],
        out_specs=pl.BlockSpec((WINDOW, B_WIDTH), cur),
    )
    return pl.pallas_call(
        _swa_prompt_kernel,
        out_shape=jax.ShapeDtypeStruct((BATCH * SEQ, B_WIDTH), BF16),
        grid_spec=grid_spec,
        compiler_params=_params(2),
        name="swa_prompt",
    )(sinks, q, k, k, v, v, _swa_prompt_bias())


SWA_DEC_TILE = 16


def _swa_decode_kernel(bias_ref, sink_ref, q_ref, knew_ref, vnew_ref, kbuf_ref, vbuf_ref,
                       o_ref, kout_ref, vout_ref):
    w = WINDOW
    last = lax.broadcasted_iota(jnp.int32, (w, B_KV_W), 0) == w - 1
    group_bits = B_GROUP.bit_length() - 1
    dim_bits = B_HEAD_DIM.bit_length() - 1
    own = (jnp.right_shift(lax.broadcasted_iota(jnp.int32, (B_HEADS, B_KV_W), 0), group_bits)
           == jnp.right_shift(lax.broadcasted_iota(jnp.int32, (B_HEADS, B_KV_W), 1), dim_bits))
    scores, vals = [], []
    for i in range(SWA_DEC_TILE):
        kb = jnp.where(last, knew_ref[i:i + 1, :], pltpu.roll(kbuf_ref[i], w - 1, 0))
        vb = jnp.where(last, vnew_ref[i:i + 1, :], pltpu.roll(vbuf_ref[i], w - 1, 0))
        kout_ref[i] = kb
        vout_ref[i] = vb
        q_wide = jnp.concatenate([q_ref[i]] * B_KV_HEADS, 1)
        q_bd = jnp.where(own, q_wide, 0.0).astype(BF16)
        scores.append(_dot_nt(q_bd, kb.astype(BF16)))
        vals.append(vb.astype(BF16))
    s = jnp.concatenate(scores, 0) + bias_ref[...]
    sk = sink_ref[...]
    m = jnp.maximum(jnp.max(s, -1, keepdims=True), sk)
    p = jnp.exp(s - m)
    den = jnp.sum(p, -1, keepdims=True) + jnp.exp(sk - m)
    pn = (p * (1.0 / den)).astype(BF16)
    kv_of_row = jnp.right_shift(
        lax.broadcasted_iota(jnp.int32, (B_HEADS, B_HEAD_DIM), 0), group_bits)
    for i in range(SWA_DEC_TILE):
        o_all = _dot(pn[i * B_HEADS:(i + 1) * B_HEADS], vals[i])
        o = jnp.zeros((B_HEADS, B_HEAD_DIM), F32)
        for kvh in range(B_KV_HEADS):
            part = o_all[:, kvh * B_HEAD_DIM:(kvh + 1) * B_HEAD_DIM]
            o = o + jnp.where(kv_of_row == kvh, part, 0.0)
        o_ref[i] = o


def _swa_decode(bias, sinks, q, knew, vnew, kbuf, vbuf):
    t = SWA_DEC_TILE
    head_tab = pl.BlockSpec((t * B_HEADS, LANE), lambda i: (0, 0))
    qo = pl.BlockSpec((t, B_HEADS, B_HEAD_DIM), lambda i: (i, 0, 0))
    new = pl.BlockSpec((t, B_KV_W), lambda i: (i, 0))
    buf = pl.BlockSpec((t, WINDOW, B_KV_W), lambda i: (i, 0, 0))
    return pl.pallas_call(
        _swa_decode_kernel,
        out_shape=[jax.ShapeDtypeStruct((DEC_BATCH, B_HEADS, B_HEAD_DIM), F32),
                   jax.ShapeDtypeStruct(kbuf.shape, F32),
                   jax.ShapeDtypeStruct(vbuf.shape, F32)],
        grid=(DEC_BATCH // t,),
        in_specs=[head_tab, head_tab, qo, new, new, buf, buf],
        out_specs=[qo, buf, buf],
        compiler_params=_params(1),
        name="swa_decode",
    )(bias, sinks, q, knew, vnew, kbuf, vbuf)


def _rope_tables(pos):
    inv = jnp.power(ROPE_THETA, -jnp.arange(0, A_ROPE, 2, dtype=F32) / A_ROPE)
    ang = pos.astype(F32)[:, None] * inv[None, :]
    zero = jnp.zeros((pos.shape[0], LANE - A_ROPE), F32)
    cos, sin = jnp.cos(ang), jnp.sin(ang)
    return jnp.concatenate([cos, cos, zero], -1), jnp.concatenate([sin, sin, zero], -1)


def _with_rot(w):
    half = A_ROPE // 2
    return jnp.concatenate([w, -w[..., half:], w[..., :half]], -1)


def kernel(x_prompt, x_sample, cache_mla, state_swa_k, state_swa_v, page_table, c_prompt, c_sample,
           w_ada, b_ada, wa_in, ga_q, ga_kv, wa_uq, wa_uk, wa_uv, wa_o,
           wb_in, bb_in, wb_sinks, wb_o, bb_o, g_final):
    n_p = BATCH * SEQ
    xp = x_prompt.reshape(n_p, D_MODEL)
    xs = x_sample.reshape(DEC_BATCH, D_MODEL)

    pad = jnp.zeros((8 - BATCH, D_MODEL), F32)
    mod = _ada_mod(jnp.concatenate([c_sample, c_prompt, pad], 0), w_ada, b_ada)

    def mods(i):
        parts = jnp.split(mod[i], 3, -1)
        sample = [p[:DEC_BATCH] for p in parts]
        prompt = [p[DEC_BATCH:DEC_BATCH + BATCH].reshape(BATCH, 1, D_MODEL) for p in parts]
        return prompt, sample

    (shift_p, scale_p, gate_p), (shift_s, scale_s, gate_s) = mods(0)
    o_pe = A_Q_LORA + A_KV_LORA
    w_in = wa_in[0]
    win = jnp.concatenate([w_in[:, :o_pe], _with_rot(w_in[:, o_pe:o_pe + A_ROPE]),
                           w_in[:, o_pe + A_ROPE:]], -1).astype(BF16)
    wuq3 = wa_uq[0].reshape(A_Q_LORA, A_HEADS, A_QK)
    wuq = jnp.concatenate([wuq3[..., :A_NOPE], _with_rot(wuq3[..., A_NOPE:])], -1).reshape(
        A_Q_LORA, A_HEADS * A_HEAD_PAD).astype(BF16)
    wuk = wa_uk[0].reshape(A_KV_LORA, A_HEADS * A_NOPE).astype(BF16)
    wuv_t = wa_uv[0].reshape(A_KV_LORA, A_WIDTH).T.astype(BF16)
    wuk_t = wa_uk[0].transpose(1, 2, 0).astype(BF16)
    wuv_h = wa_uv[0].transpose(1, 0, 2).astype(BF16)
    gq = ga_q[0].reshape(1, A_Q_LORA)
    gkv = ga_kv[0].reshape(1, A_KV_LORA)
    wo_a = wa_o[0].astype(BF16)
    zero_bias = jnp.zeros((1, D_MODEL), F32)

    tm_p = 256
    tiles_p = SEQ // tm_p
    cc_p, ss_p = _rope_tables(jnp.arange(SEQ))
    z_p, rows_p, qcat, kcat, v_p = _mla_proj(
        False, xp, shift_p, scale_p, cc_p, ss_p, win, gq, gkv, wuq, wuk, wuv_t, tm_p, tiles_p)
    o_p = _flash(qcat, kcat, v_p)
    xp = _outproj(False, o_p, z_p, wo_a, zero_bias, xp, gate_p, None, False, tm_p, tiles_p)

    cc_s, ss_s = _rope_tables(jnp.full((DEC_BATCH,), PAST_LEN))
    z_s, rows_s, qabs, qpe = _mla_proj(
        True, xs, shift_s, scale_s, cc_s, ss_s, win, gq, gkv, wuq, wuk_t, wuv_t, DEC_BATCH, 1)
    o_lat = _mla_decode(page_table,
                        qabs.reshape(DEC_BATCH, A_HEADS, A_KV_LORA),
                        qpe.reshape(DEC_BATCH, A_HEADS, LANE),
                        rows_s.reshape(DEC_BATCH, 1, A_CACHE_W), jnp.swapaxes(cache_mla, 2, 3))
    o_s = _mla_decode_uv(o_lat.reshape(DEC_BATCH, A_HEADS * A_KV_LORA), wuv_h)
    xs = _outproj(False, o_s, z_s, wo_a, zero_bias, xs, gate_s, None, True, DEC_BATCH, 1)

    (shift_p, scale_p, gate_p), (shift_s, scale_s, gate_s) = mods(1)
    w_b = wb_in[0].astype(BF16)
    b_b = bb_in[0].reshape(1, B_IN)
    wo_b = wb_o[0].astype(BF16)
    bo_b = bb_o[0].reshape(1, D_MODEL)
    gfin = g_final.reshape(1, D_MODEL)
    sinks = wb_sinks[0]

    q_p, k_p, v_p, z_p = _swa_proj(xp, shift_p, scale_p, w_b, b_b, False, tm_p, tiles_p)
    o_p = _swa_prompt(sinks, q_p, k_p, v_p)
    y_p = _outproj(True, o_p, z_p, wo_b, bo_b, xp, gate_p, gfin, False, tm_p, tiles_p)

    q_s, k_s, v_s, z_s = _swa_proj(xs, shift_s, scale_s, w_b, b_b, True, DEC_BATCH, 1)
    slopes = jnp.power(2.0, -8.0 * jnp.arange(1, B_HEADS + 1, dtype=F32) / B_HEADS)
    behind = (WINDOW - 1 - jnp.arange(WINDOW)).astype(F32)
    bias_s = jnp.tile(-slopes[:, None] * behind[None, :], (SWA_DEC_TILE, 1))
    sink_s = jnp.tile(jnp.broadcast_to(sinks[:, None], (B_HEADS, LANE)), (SWA_DEC_TILE, 1))
    o_s, kwin, vwin = _swa_decode(
        bias_s, sink_s, q_s.astype(F32).reshape(DEC_BATCH, B_HEADS, B_HEAD_DIM), k_s, v_s,
        state_swa_k[0].reshape(DEC_BATCH, WINDOW, B_KV_W),
        state_swa_v[0].reshape(DEC_BATCH, WINDOW, B_KV_W))
    y_s = _outproj(True, o_s.reshape(DEC_BATCH, B_WIDTH), z_s, wo_b, bo_b, xs, gate_s, gfin,
                   True, DEC_BATCH, 1)

    kv_shape = (1, BATCH, WINDOW, B_KV_HEADS, B_HEAD_DIM)
    win_shape = (1, DEC_BATCH, WINDOW, B_KV_HEADS, B_HEAD_DIM)
    return (y_p.reshape(BATCH, SEQ, D_MODEL),
            y_s.reshape(DEC_BATCH, 1, D_MODEL),
            rows_p.reshape(1, BATCH, SEQ, A_CACHE_W),
            rows_s.reshape(1, DEC_BATCH, 1, A_CACHE_W),
            k_p.reshape(BATCH, SEQ, B_KV_W)[:, SEQ - WINDOW:].reshape(kv_shape),
            v_p.reshape(BATCH, SEQ, B_KV_W)[:, SEQ - WINDOW:].reshape(kv_shape),
            kwin.reshape(win_shape),
            vwin.reshape(win_shape))
```

```python
import functools

import jax
import jax.numpy as jnp
from jax import lax
from jax.experimental import pallas as pl
from jax.experimental.pallas import tpu as pltpu

F32 = jnp.float32
BF16 = jnp.bfloat16

D_MODEL = 2048
BATCH = 2
SEQ = 4096
DEC_BATCH = 128
PAST_LEN = 16384
PAGE_SIZE = 128
N_PAGES = PAST_LEN // PAGE_SIZE

A_HEADS = 16
A_Q_LORA = 512
A_KV_LORA = 512
A_NOPE = 128
A_ROPE = 64
A_VDIM = 128
A_QK = A_NOPE + A_ROPE
A_WIDTH = A_HEADS * A_VDIM
A_CACHE_W = A_KV_LORA + A_ROPE
A_HEAD_PAD = 256
A_IN_EXT = A_Q_LORA + A_KV_LORA + 2 * A_ROPE + A_WIDTH
ROPE_THETA = 10000.0

B_HEADS = 32
B_KV_HEADS = 4
B_GROUP = 8
B_HEAD_DIM = 64
B_WIDTH = B_HEADS * B_HEAD_DIM
B_KV_W = B_KV_HEADS * B_HEAD_DIM
B_IN = 2 * B_WIDTH + 2 * B_KV_W
WINDOW = 128

NORM_EPS = 1e-6
NEG_INF = -1e30
A_SCALE = A_QK ** -0.5
B_SCALE = B_HEAD_DIM ** -0.5
LOG2_E = 1.4426950408889634

LANE = 128
VMEM_LIMIT = 56 * 1024 * 1024

_NT = (((1,), (1,)), ((), ()))


def _params(n_grid):
    return pltpu.CompilerParams(
        dimension_semantics=("arbitrary",) * n_grid, vmem_limit_bytes=VMEM_LIMIT)


def _resident(shape):
    nd = len(shape)
    return pl.BlockSpec(shape, lambda *_: (0,) * nd, pipeline_mode=pl.Buffered(1))


def _rms(x):
    return x * lax.rsqrt(jnp.mean(x * x, -1, keepdims=True) + NORM_EPS)


def _silu(x):
    return x * jax.nn.sigmoid(x)


def _dot(a, b):
    return jnp.dot(a, b, preferred_element_type=F32)


def _dot_nt(a, b):
    return lax.dot_general(a, b, _NT, preferred_element_type=F32)


def _ada_kernel(c_ref, w_ref, b_ref, o_ref):
    a = _silu(c_ref[...]).astype(BF16)
    o_ref[...] = _dot(a, w_ref[...].astype(BF16)) + b_ref[...]


def _ada_mod(c_all, w_ada, b_ada):
    depth, _, n = w_ada.shape
    m = c_all.shape[0]
    tn = 1024
    return pl.pallas_call(
        _ada_kernel,
        out_shape=jax.ShapeDtypeStruct((depth, m, n), F32),
        grid=(depth, n // tn),
        in_specs=[
            pl.BlockSpec((m, D_MODEL), lambda i, j: (0, 0)),
            pl.BlockSpec((None, D_MODEL, tn), lambda i, j: (i, 0, j)),
            pl.BlockSpec((None, 1, tn), lambda i, j: (i, 0, j)),
        ],
        out_specs=pl.BlockSpec((None, m, tn), lambda i, j: (i, 0, j)),
        compiler_params=_params(2),
        name="ada_mod",
    )(c_all, w_ada, b_ada.reshape(depth, 1, n))


def _mod_specs(per_token, tm, tiles_per_seq):
    if per_token:
        return pl.BlockSpec((tm, D_MODEL), lambda i: (i, 0))
    return pl.BlockSpec((None, 1, D_MODEL), lambda i: (i // tiles_per_seq, 0, 0))


def _rope_slab(slab, cc, ss):
    return slab * cc + pltpu.roll(slab, A_ROPE, 1) * ss


def _mla_proj_kernel(decode, x_ref, shift_ref, scale_ref, cc_ref, ss_ref, win_ref, gq_ref,
                     gkv_ref, wuq_ref, wa_ref, wb_ref, *outs):
    if decode:
        z_ref, rows_ref, qabs_ref, qpe_ref = outs
    else:
        z_ref, rows_ref, qcat_ref, kcat_ref, v_ref = outs
    h = (_rms(x_ref[...]) * (1.0 + scale_ref[...]) + shift_ref[...]).astype(BF16)
    o_q, o_kv, o_pe, o_z = 0, A_Q_LORA, A_Q_LORA + A_KV_LORA, A_Q_LORA + A_KV_LORA + 2 * A_ROPE
    q_lat = _dot(h, win_ref[:, o_q:o_kv])
    kv_lat = _dot(h, win_ref[:, o_kv:o_pe])
    pe2 = _dot(h, win_ref[:, o_pe:o_z])
    z_ref[...] = _dot(h, win_ref[:, o_z:])
    cc = cc_ref[...]
    ss = ss_ref[...]
    kv = _rms(kv_lat) * gkv_ref[...]
    kpe = _rope_slab(pe2, cc, ss)
    rows_ref[:, :A_KV_LORA] = kv
    rows_ref[:, A_KV_LORA:] = kpe[:, :A_ROPE]
    qn = (_rms(q_lat) * gq_ref[...]).astype(BF16)
    q_scale = A_SCALE if decode else A_SCALE * LOG2_E
    if not decode:
        kvb = kv.astype(BF16)
        kpe_b = kpe.astype(BF16)
    for hd in range(A_HEADS):
        lo = hd * A_HEAD_PAD
        qh = _dot(qn, wuq_ref[:, lo:lo + A_HEAD_PAD])
        nope = qh[:, :A_NOPE]
        qpe = (_rope_slab(qh[:, A_NOPE:], cc, ss) * q_scale).astype(BF16)
        if decode:
            qabs = _dot(nope.astype(BF16), wa_ref[hd])
            qabs_ref[:, hd * A_KV_LORA:(hd + 1) * A_KV_LORA] = (qabs * A_SCALE).astype(BF16)
            qpe_ref[:, hd * LANE:(hd + 1) * LANE] = qpe
        else:
            qcat_ref[:, lo:lo + A_NOPE] = (nope * q_scale).astype(BF16)
            qcat_ref[:, lo + A_NOPE:lo + A_HEAD_PAD] = qpe
            knope = _dot(kvb, wa_ref[:, hd * A_NOPE:(hd + 1) * A_NOPE])
            kcat_ref[:, lo:lo + A_NOPE] = knope.astype(BF16)
            kcat_ref[:, lo + A_NOPE:lo + A_HEAD_PAD] = kpe_b
            wv_t = wb_ref[hd * A_VDIM:(hd + 1) * A_VDIM, :]
            v_ref[hd] = _dot_nt(wv_t, kvb).astype(BF16)


def _mla_proj(decode, x, shift, scale, cc, ss, win, gq, gkv, wuq, wa, wb, tm, tiles_per_seq):
    m = x.shape[0]
    mod_spec = _mod_specs(decode, tm, tiles_per_seq)
    tab_spec = pl.BlockSpec((tm, LANE), lambda i: (i % tiles_per_seq, 0))
    row = lambda n: pl.BlockSpec((tm, n), lambda i: (i, 0))
    out_shape = [jax.ShapeDtypeStruct((m, A_WIDTH), F32),
                 jax.ShapeDtypeStruct((m, A_CACHE_W), F32)]
    out_specs = [row(A_WIDTH), row(A_CACHE_W)]
    if decode:
        out_shape += [jax.ShapeDtypeStruct((m, A_HEADS * A_KV_LORA), BF16),
                      jax.ShapeDtypeStruct((m, A_HEADS * LANE), BF16)]
        out_specs += [row(A_HEADS * A_KV_LORA), row(A_HEADS * LANE)]
    else:
        out_shape += [jax.ShapeDtypeStruct((m, A_HEADS * A_HEAD_PAD), BF16),
                      jax.ShapeDtypeStruct((m, A_HEADS * A_HEAD_PAD), BF16),
                      jax.ShapeDtypeStruct(
                          (BATCH, A_HEADS, SEQ // FLASH_TK, A_VDIM, FLASH_TK), BF16)]
        per_blk = FLASH_TK // tm

        def v_map(i):
            t = i % tiles_per_seq
            return (i // tiles_per_seq, 0, t // per_blk, 0, t % per_blk)

        out_specs += [row(A_HEADS * A_HEAD_PAD), row(A_HEADS * A_HEAD_PAD),
                      pl.BlockSpec((None, A_HEADS, None, A_VDIM, tm), v_map)]
    return pl.pallas_call(
        functools.partial(_mla_proj_kernel, decode),
        out_shape=out_shape,
        grid=(m // tm,),
        in_specs=[row(D_MODEL), mod_spec, mod_spec, tab_spec, tab_spec,
                  _resident(win.shape), _resident(gq.shape), _resident(gkv.shape),
                  _resident(wuq.shape), _resident(wa.shape), _resident(wb.shape)],
        out_specs=out_specs,
        compiler_params=_params(1),
        name="mla_proj_decode" if decode else "mla_proj_prompt",
    )(x, shift, scale, cc, ss, win, gq, gkv, wuq, wa, wb)


FLASH_TQ = 1024
FLASH_TK = 512
FLASH_CHAIN = 256
FLASH_DEN_ROWS = 16


def _flash_kernel(q_ref, k_ref, vt_ref, o_ref, m_ref, acc_ref, sa_ref, sb_ref):
    tq, tk, cw = FLASH_TQ, FLASH_TK, FLASH_CHAIN
    n_chain = tq // cw
    qi = pl.program_id(2)
    m_ref[...] = jnp.full(m_ref.shape, NEG_INF, F32)
    acc_ref[...] = jnp.zeros(acc_ref.shape, F32)
    ones = jnp.ones((FLASH_DEN_ROWS, tk), BF16)

    def update(c, s, vt_blk):
        m_prev = m_ref[c]
        m_new = jnp.maximum(m_prev, jnp.max(s, 0, keepdims=True))
        alpha = jnp.exp2(m_prev - m_new)
        p = jnp.exp2(s - m_new).astype(BF16)
        acc_ref[c] = alpha * acc_ref[c] + _dot(vt_blk, p)
        m_ref[c] = m_new

    def visible(key_off):
        return [c for c in range(n_chain) if key_off is None or key_off <= (c + 1) * cw - 1]

    def score(ki, s_ref, key_off):
        start = pl.multiple_of(ki * tk, tk)
        k_blk = k_ref[pl.ds(start, tk), :]
        for c in visible(key_off):
            s_ref[c] = _dot_nt(k_blk, q_ref[c * cw:(c + 1) * cw, :])

    def consume(ki, s_ref, key_off):
        vt_blk = jnp.concatenate([vt_ref[ki], ones], 0)
        for c in visible(key_off):
            s = s_ref[c]
            if key_off is not None and key_off + tk - 1 > c * cw:
                key = lax.broadcasted_iota(jnp.int32, (tk, cw), 0) + key_off
                qry = lax.broadcasted_iota(jnp.int32, (tk, cw), 1) + c * cw
                s = jnp.where(key <= qry, s, NEG_INF)
            update(c, s, vt_blk)

    assert tq == 2 * tk
    score(0, sa_ref, None)

    def body(j, carry):
        score(2 * j + 1, sb_ref, None)
        consume(2 * j, sa_ref, None)
        score(2 * j + 2, sa_ref, None)
        consume(2 * j + 1, sb_ref, None)
        return carry

    lax.fori_loop(0, qi, body, 0)
    score(2 * qi + 1, sb_ref, tk)
    consume(2 * qi, sa_ref, 0)
    consume(2 * qi + 1, sb_ref, tk)
    for c in range(n_chain):
        den = acc_ref[c, A_VDIM:A_VDIM + 1, :]
        o_t = acc_ref[c, :A_VDIM, :] * (1.0 / den)
        o_ref[c * cw:(c + 1) * cw, :] = o_t.T.astype(o_ref.dtype)


def _flash(qcat, kcat, vt):
    tq, tk, cw = FLASH_TQ, FLASH_TK, FLASH_CHAIN
    nq, nk, n_chain = SEQ // tq, SEQ // tk, tq // cw
    return pl.pallas_call(
        _flash_kernel,
        out_shape=jax.ShapeDtypeStruct((BATCH * SEQ, A_WIDTH), BF16),
        grid=(BATCH, A_HEADS, nq),
        in_specs=[
            pl.BlockSpec((tq, A_HEAD_PAD), lambda b, h, i: (b * nq + i, h)),
            pl.BlockSpec((SEQ, A_HEAD_PAD), lambda b, h, i: (b, h)),
            pl.BlockSpec((None, None, nk, A_VDIM, tk), lambda b, h, i: (b, h, 0, 0, 0)),
        ],
        out_specs=pl.BlockSpec((tq, A_VDIM), lambda b, h, i: (b * nq + i, h)),
        scratch_shapes=[pltpu.VMEM((n_chain, 1, cw), F32),
                        pltpu.VMEM((n_chain, A_VDIM + FLASH_DEN_ROWS, cw), F32),
                        pltpu.VMEM((n_chain, tk, cw), F32), pltpu.VMEM((n_chain, tk, cw), F32)],
        compiler_params=_params(3),
        name="mla_flash",
    )(qcat, kcat, vt)


DEC_PAGES_PER_STEP = 32
DEC_GROUPS = 4


def _mla_decode_kernel(pt_ref, qabs_ref, qpe_ref, row_ref, cache_ref, o_ref,
                       m_ref, l_ref, acc_ref, buf_ref, sem_ref):
    n = DEC_PAGES_PER_STEP
    b = pl.program_id(0)
    j = pl.program_id(1)
    n_chunk = pl.num_programs(1)
    step = b * n_chunk + j
    slot = lax.rem(step, 2)

    def page_copy(seq, chunk, i, half):
        page = pt_ref[seq, chunk * n + i]
        return pltpu.make_async_copy(cache_ref.at[0, page], buf_ref.at[half, i], sem_ref.at[half])

    @pl.when(step == 0)
    def _():
        for i in range(n):
            page_copy(0, 0, i, 0).start()

    wraps = j + 1 == n_chunk
    next_seq = jnp.where(wraps, b + 1, b)
    next_chunk = jnp.where(wraps, 0, j + 1)

    @pl.when(next_seq < pl.num_programs(0))
    def _():
        for i in range(n):
            page_copy(next_seq, next_chunk, i, 1 - slot).start()

    for i in range(n):
        page_copy(b, j, i, slot).wait()
    page_refs = [buf_ref.at[slot, i] for i in range(n)]

    @pl.when(j == 0)
    def _():
        m_ref[...] = jnp.full(m_ref.shape, NEG_INF, F32)
        l_ref[...] = jnp.zeros(l_ref.shape, F32)
        acc_ref[...] = jnp.zeros(acc_ref.shape, F32)

    qa = qabs_ref[...]
    qp = qpe_ref[:, :A_ROPE]
    per_group = n // DEC_GROUPS

    def score(g):
        refs = page_refs[g * per_group:(g + 1) * per_group]
        pages = [r[...].astype(BF16) for r in refs]
        kv_t = jnp.concatenate([pg[:A_KV_LORA] for pg in pages], 1)
        pe_t = jnp.concatenate([pg[A_KV_LORA:] for pg in pages], 1)
        return _dot(qa, kv_t) + _dot(qp, pe_t), kv_t

    def update(s, kv_t):
        m_prev = m_ref[...]
        m_new = jnp.maximum(m_prev, jnp.max(s, -1, keepdims=True))
        alpha = jnp.exp(m_prev - m_new)
        p = jnp.exp(s - m_new)
        l_ref[...] = alpha * l_ref[...] + jnp.sum(p, -1, keepdims=True)
        acc_ref[...] = alpha * acc_ref[...] + _dot_nt(p.astype(BF16), kv_t)
        m_ref[...] = m_new

    pending = score(0)
    for g in range(1, DEC_GROUPS):
        ahead = score(g)
        update(*pending)
        pending = ahead
    update(*pending)

    @pl.when(j == pl.num_programs(1) - 1)
    def _():
        rb = row_ref[...].astype(BF16).astype(F32)
        s_new = (jnp.sum(qa.astype(F32) * rb[:, :A_KV_LORA], -1, keepdims=True)
                 + jnp.sum(qp.astype(F32) * rb[:, A_KV_LORA:], -1, keepdims=True))
        m_old = m_ref[...]
        m_fin = jnp.maximum(m_old, s_new)
        corr = jnp.exp(m_old - m_fin)
        p_new = jnp.exp(s_new - m_fin)
        l_fin = l_ref[...] * corr + p_new
        acc_fin = acc_ref[...] * corr + p_new.astype(BF16).astype(F32) * rb[:, :A_KV_LORA]
        o_ref[...] = acc_fin / l_fin


def _mla_decode(page_table, qabs, qpe, rows, cache):
    n = DEC_PAGES_PER_STEP
    grid_spec = pltpu.PrefetchScalarGridSpec(
        num_scalar_prefetch=1,
        grid=(DEC_BATCH, N_PAGES // n),
        in_specs=[
            pl.BlockSpec((None, A_HEADS, A_KV_LORA), lambda b, j, pt: (b, 0, 0)),
            pl.BlockSpec((None, A_HEADS, LANE), lambda b, j, pt: (b, 0, 0)),
            pl.BlockSpec((None, 1, A_CACHE_W), lambda b, j, pt: (b, 0, 0)),
            pl.BlockSpec(memory_space=pl.ANY),
        ],
        out_specs=pl.BlockSpec((None, A_HEADS, A_KV_LORA), lambda b, j, pt: (b, 0, 0)),
        scratch_shapes=[pltpu.VMEM((A_HEADS, 1), F32), pltpu.VMEM((A_HEADS, 1), F32),
                        pltpu.VMEM((A_HEADS, A_KV_LORA), F32),
                        pltpu.VMEM((2, n, A_CACHE_W, PAGE_SIZE), F32),
                        pltpu.SemaphoreType.DMA((2,))],
    )
    return pl.pallas_call(
        _mla_decode_kernel,
        out_shape=jax.ShapeDtypeStruct((DEC_BATCH, A_HEADS, A_KV_LORA), F32),
        grid_spec=grid_spec,
        compiler_params=_params(2),
        name="mla_decode",
    )(page_table, qabs, qpe, rows, cache)


def _uv_kernel(o_ref, w_ref, out_ref):
    out_ref[...] = _dot(o_ref[...].astype(BF16), w_ref[...]).astype(out_ref.dtype)


def _mla_decode_uv(o_lat2d, wuv_heads):
    m = o_lat2d.shape[0]
    return pl.pallas_call(
        _uv_kernel,
        out_shape=jax.ShapeDtypeStruct((m, A_WIDTH), BF16),
        grid=(A_HEADS,),
        in_specs=[pl.BlockSpec((m, A_KV_LORA), lambda h: (0, h)),
                  pl.BlockSpec((None, A_KV_LORA, A_VDIM), lambda h: (h, 0, 0))],
        out_specs=pl.BlockSpec((m, A_VDIM), lambda h: (0, h)),
        compiler_params=_params(1),
        name="mla_decode_uv",
    )(o_lat2d, wuv_heads)


def _outproj_kernel(final, o_ref, z_ref, w_ref, b_ref, x_ref, gate_ref, *rest):
    g = (o_ref[...].astype(F32) * _silu(z_ref[...])).astype(BF16)
    y = _dot(g, w_ref[...]) + b_ref[...]
    x_new = x_ref[...] + gate_ref[...] * y
    if final:
        gfin_ref, out_ref = rest
        out_ref[...] = _rms(x_new) * gfin_ref[...]
    else:
        (out_ref,) = rest
        out_ref[...] = x_new


def _outproj(final, o, z, w, b, x, gate, gfin, per_token, tm, tiles_per_seq):
    m = x.shape[0]
    row = lambda n: pl.BlockSpec((tm, n), lambda i: (i, 0))
    in_specs = [row(o.shape[1]), row(z.shape[1]), _resident(w.shape), _resident(b.shape),
                row(D_MODEL), _mod_specs(per_token, tm, tiles_per_seq)]
    args = [o, z, w, b, x, gate]
    if final:
        in_specs.append(_resident(gfin.shape))
        args.append(gfin)
    return pl.pallas_call(
        functools.partial(_outproj_kernel, final),
        out_shape=jax.ShapeDtypeStruct((m, D_MODEL), F32),
        grid=(m // tm,),
        in_specs=in_specs,
        out_specs=row(D_MODEL),
        compiler_params=_params(1),
        name="outproj_final" if final else "outproj",
    )(*args)


def _swa_proj_kernel(x_ref, shift_ref, scale_ref, w_ref, b_ref, q_ref, k_ref, v_ref, z_ref):
    h = (_rms(x_ref[...]) * (1.0 + scale_ref[...]) + shift_ref[...]).astype(BF16)
    o_k, o_v, o_z = B_WIDTH, B_WIDTH + B_KV_W, B_WIDTH + 2 * B_KV_W
    q = _dot(h, w_ref[:, :o_k]) + b_ref[:, :o_k]
    q_ref[...] = (q * B_SCALE).astype(BF16)
    k_ref[...] = _dot(h, w_ref[:, o_k:o_v]) + b_ref[:, o_k:o_v]
    v_ref[...] = _dot(h, w_ref[:, o_v:o_z]) + b_ref[:, o_v:o_z]
    z_ref[...] = _dot(h, w_ref[:, o_z:]) + b_ref[:, o_z:]


def _swa_proj(x, shift, scale, w, b, per_token, tm, tiles_per_seq):
    m = x.shape[0]
    row = lambda n: pl.BlockSpec((tm, n), lambda i: (i, 0))
    mod_spec = _mod_specs(per_token, tm, tiles_per_seq)
    return pl.pallas_call(
        _swa_proj_kernel,
        out_shape=[jax.ShapeDtypeStruct((m, B_WIDTH), BF16),
                   jax.ShapeDtypeStruct((m, B_KV_W), F32),
                   jax.ShapeDtypeStruct((m, B_KV_W), F32),
                   jax.ShapeDtypeStruct((m, B_WIDTH), F32)],
        grid=(m // tm,),
        in_specs=[row(D_MODEL), mod_spec, mod_spec, _resident(w.shape), _resident(b.shape)],
        out_specs=[row(B_WIDTH), row(B_KV_W), row(B_KV_W), row(B_WIDTH)],
        compiler_params=_params(1),
        name="swa_proj",
    )(x, shift, scale, w, b)


def _block_diag2(t):
    zero = jnp.zeros_like(t)
    return jnp.concatenate([jnp.concatenate([t, zero], 1), jnp.concatenate([zero, t], 1)], 0)


def _swa_prompt_kernel(sink_ref, q_ref, kc_ref, kp_ref, vc_ref, vp_ref, bias_ref, o_ref):
    w = WINDOW
    kband = jnp.concatenate([kp_ref[...], kc_ref[...]], 0).astype(BF16)
    vband = jnp.concatenate([vp_ref[...], vc_ref[...]], 0).astype(BF16)
    pair_w = 2 * B_HEAD_DIM
    n_pair = B_HEADS // 2
    pairs_per_kv = B_GROUP // 2
    scores, vbs = [], []
    for kvh in range(B_KV_HEADS):
        kb = _block_diag2(kband[:, kvh * B_HEAD_DIM:(kvh + 1) * B_HEAD_DIM])
        vbs.append(_block_diag2(vband[:, kvh * B_HEAD_DIM:(kvh + 1) * B_HEAD_DIM]))
        for gp in range(pairs_per_kv):
            lo = (kvh * pairs_per_kv + gp) * pair_w
            scores.append(_dot_nt(q_ref[:, lo:lo + pair_w], kb))
    probs = []
    for idx in range(n_pair):
        both = []
        for t in range(2):
            head = 2 * idx + t
            s = scores[idx][:, t * 2 * w:(t + 1) * 2 * w] + bias_ref[head]
            sk = sink_ref[head]
            m = jnp.maximum(jnp.max(s, -1, keepdims=True), sk)
            p = jnp.exp(s - m)
            den = jnp.sum(p, -1, keepdims=True) + jnp.exp(sk - m)
            both.append((p * (1.0 / den)).astype(BF16))
        probs.append(jnp.concatenate(both, 1))
    for idx in range(n_pair):
        o2 = _dot(probs[idx], vbs[idx // pairs_per_kv])
        o_ref[:, idx * pair_w:(idx + 1) * pair_w] = o2.astype(o_ref.dtype)


def _swa_prompt_bias():
    i = jnp.arange(WINDOW)[:, None]
    j = jnp.arange(2 * WINDOW)[None, :]
    dist = i - j + WINDOW
    valid = (dist >= 0) & (dist < WINDOW)
    slopes = jnp.power(2.0, -8.0 * jnp.arange(1, B_HEADS + 1, dtype=F32) / B_HEADS)
    alibi = -slopes[:, None, None] * dist.astype(F32)
    later = jnp.where(valid[None], alibi, NEG_INF)
    first = jnp.where((valid & (j >= WINDOW))[None], alibi, NEG_INF)
    return jnp.stack([first, later])


def _swa_prompt(sinks, q, k, v):
    nb = SEQ // WINDOW
    cur = lambda b, n, s: (b * nb + n, 0)
    prev = lambda b, n, s: (b * nb + jnp.maximum(n - 1, 0), 0)
    grid_spec = pltpu.PrefetchScalarGridSpec(
        num_scalar_prefetch=1,
        grid=(BATCH, nb),
        in_specs=[pl.BlockSpec((WINDOW, B_WIDTH), cur),
                  pl.BlockSpec((WINDOW, B_KV_W), cur), pl.BlockSpec((WINDOW, B_KV_W), prev),
                  pl.BlockSpec((WINDOW, B_KV_W), cur), pl.BlockSpec((WINDOW, B_KV_W), prev),
                  pl.BlockSpec((None, B_HEADS, WINDOW, 2 * WINDOW),
                               lambda b, n, s: (jnp.minimum(n, 1), 0, 0, 0))],
        out_specs=pl.BlockSpec((WINDOW, B_WIDTH), cur),
    )
    return pl.pallas_call(
        _swa_prompt_kernel,
        out_shape=jax.ShapeDtypeStruct((BATCH * SEQ, B_WIDTH), BF16),
        grid_spec=grid_spec,
        compiler_params=_params(2),
        name="swa_prompt",
    )(sinks, q, k, k, v, v, _swa_prompt_bias())


SWA_DEC_TILE = 16


def _swa_decode_kernel(bias_ref, sink_ref, q_ref, knew_ref, vnew_ref, kbuf_ref, vbuf_ref,
                       o_ref, kout_ref, vout_ref):
    w = WINDOW
    last = lax.broadcasted_iota(jnp.int32, (w, B_KV_W), 0) == w - 1
    group_bits = B_GROUP.bit_length() - 1
    dim_bits = B_HEAD_DIM.bit_length() - 1
    own = (jnp.right_shift(lax.broadcasted_iota(jnp.int32, (B_HEADS, B_KV_W), 0), group_bits)
           == jnp.right_shift(lax.broadcasted_iota(jnp.int32, (B_HEADS, B_KV_W), 1), dim_bits))
    scores, vals = [], []
    for i in range(SWA_DEC_TILE):
        kb = jnp.where(last, knew_ref[i:i + 1, :], pltpu.roll(kbuf_ref[i], w - 1, 0))
        vb = jnp.where(last, vnew_ref[i:i + 1, :], pltpu.roll(vbuf_ref[i], w - 1, 0))
        kout_ref[i] = kb
        vout_ref[i] = vb
        q_wide = jnp.concatenate([q_ref[i]] * B_KV_HEADS, 1)
        q_bd = jnp.where(own, q_wide, 0.0).astype(BF16)
        scores.append(_dot_nt(q_bd, kb.astype(BF16)))
        vals.append(vb.astype(BF16))
    s = jnp.concatenate(scores, 0) + bias_ref[...]
    sk = sink_ref[...]
    m = jnp.maximum(jnp.max(s, -1, keepdims=True), sk)
    p = jnp.exp(s - m)
    den = jnp.sum(p, -1, keepdims=True) + jnp.exp(sk - m)
    pn = (p * (1.0 / den)).astype(BF16)
    kv_of_row = jnp.right_shift(
        lax.broadcasted_iota(jnp.int32, (B_HEADS, B_HEAD_DIM), 0), group_bits)
    for i in range(SWA_DEC_TILE):
        o_all = _dot(pn[i * B_HEADS:(i + 1) * B_HEADS], vals[i])
        o = jnp.zeros((B_HEADS, B_HEAD_DIM), F32)
        for kvh in range(B_KV_HEADS):
            part = o_all[:, kvh * B_HEAD_DIM:(kvh + 1) * B_HEAD_DIM]
            o = o + jnp.where(kv_of_row == kvh, part, 0.0)
        o_ref[i] = o


def _swa_decode(bias, sinks, q, knew, vnew, kbuf, vbuf):
    t = SWA_DEC_TILE
    head_tab = pl.BlockSpec((t * B_HEADS, LANE), lambda i: (0, 0))
    qo = pl.BlockSpec((t, B_HEADS, B_HEAD_DIM), lambda i: (i, 0, 0))
    new = pl.BlockSpec((t, B_KV_W), lambda i: (i, 0))
    buf = pl.BlockSpec((t, WINDOW, B_KV_W), lambda i: (i, 0, 0))
    return pl.pallas_call(
        _swa_decode_kernel,
        out_shape=[jax.ShapeDtypeStruct((DEC_BATCH, B_HEADS, B_HEAD_DIM), F32),
                   jax.ShapeDtypeStruct(kbuf.shape, F32),
                   jax.ShapeDtypeStruct(vbuf.shape, F32)],
        grid=(DEC_BATCH // t,),
        in_specs=[head_tab, head_tab, qo, new, new, buf, buf],
        out_specs=[qo, buf, buf],
        compiler_params=_params(1),
        name="swa_decode",
    )(bias, sinks, q, knew, vnew, kbuf, vbuf)


def _rope_tables(pos):
    inv = jnp.power(ROPE_THETA, -jnp.arange(0, A_ROPE, 2, dtype=F32) / A_ROPE)
    ang = pos.astype(F32)[:, None] * inv[None, :]
    zero = jnp.zeros((pos.shape[0], LANE - A_ROPE), F32)
    cos, sin = jnp.cos(ang), jnp.sin(ang)
    return jnp.concatenate([cos, cos, zero], -1), jnp.concatenate([sin, sin, zero], -1)


def _with_rot(w):
    half = A_ROPE // 2
    return jnp.concatenate([w, -w[..., half:], w[..., :half]], -1)


def kernel(x_prompt, x_sample, cache_mla, state_swa_k, state_swa_v, page_table, c_prompt, c_sample,
           w_ada, b_ada, wa_in, ga_q, ga_kv, wa_uq, wa_uk, wa_uv, wa_o,
           wb_in, bb_in, wb_sinks, wb_o, bb_o, g_final):
    n_p = BATCH * SEQ
    xp = x_prompt.reshape(n_p, D_MODEL)
    xs = x_sample.reshape(DEC_BATCH, D_MODEL)

    pad = jnp.zeros((8 - BATCH, D_MODEL), F32)
    mod = _ada_mod(jnp.concatenate([c_sample, c_prompt, pad], 0), w_ada, b_ada)

    def mods(i):
        parts = jnp.split(mod[i], 3, -1)
        sample = [p[:DEC_BATCH] for p in parts]
        prompt = [p[DEC_BATCH:DEC_BATCH + BATCH].reshape(BATCH, 1, D_MODEL) for p in parts]
        return prompt, sample

    (shift_p, scale_p, gate_p), (shift_s, scale_s, gate_s) = mods(0)
    o_pe = A_Q_LORA + A_KV_LORA
    w_in = wa_in[0]
    win = jnp.concatenate([w_in[:, :o_pe], _with_rot(w_in[:, o_pe:o_pe + A_ROPE]),
                           w_in[:, o_pe + A_ROPE:]], -1).astype(BF16)
    wuq3 = wa_uq[0].reshape(A_Q_LORA, A_HEADS, A_QK)
    wuq = jnp.concatenate([wuq3[..., :A_NOPE], _with_rot(wuq3[..., A_NOPE:])], -1).reshape(
        A_Q_LORA, A_HEADS * A_HEAD_PAD).astype(BF16)
    wuk = wa_uk[0].reshape(A_KV_LORA, A_HEADS * A_NOPE).astype(BF16)
    wuv_t = wa_uv[0].reshape(A_KV_LORA, A_WIDTH).T.astype(BF16)
    wuk_t = wa_uk[0].transpose(1, 2, 0).astype(BF16)
    wuv_h = wa_uv[0].transpose(1, 0, 2).astype(BF16)
    gq = ga_q[0].reshape(1, A_Q_LORA)
    gkv = ga_kv[0].reshape(1, A_KV_LORA)
    wo_a = wa_o[0].astype(BF16)
    zero_bias = jnp.zeros((1, D_MODEL), F32)

    tm_p = 256
    tiles_p = SEQ // tm_p
    cc_p, ss_p = _rope_tables(jnp.arange(SEQ))
    z_p, rows_p, qcat, kcat, v_p = _mla_proj(
        False, xp, shift_p, scale_p, cc_p, ss_p, win, gq, gkv, wuq, wuk, wuv_t, tm_p, tiles_p)
    o_p = _flash(qcat, kcat, v_p)
    xp = _outproj(False, o_p, z_p, wo_a, zero_bias, xp, gate_p, None, False, tm_p, tiles_p)

    cc_s, ss_s = _rope_tables(jnp.full((DEC_BATCH,), PAST_LEN))
    z_s, rows_s, qabs, qpe = _mla_proj(
        True, xs, shift_s, scale_s, cc_s, ss_s, win, gq, gkv, wuq, wuk_t, wuv_t, DEC_BATCH, 1)
    o_lat = _mla_decode(page_table,
                        qabs.reshape(DEC_BATCH, A_HEADS, A_KV_LORA),
                        qpe.reshape(DEC_BATCH, A_HEADS, LANE),
                        rows_s.reshape(DEC_BATCH, 1, A_CACHE_W), jnp.swapaxes(cache_mla, 2, 3))
    o_s = _mla_decode_uv(o_lat.reshape(DEC_BATCH, A_HEADS * A_KV_LORA), wuv_h)
    xs = _outproj(False, o_s, z_s, wo_a, zero_bias, xs, gate_s, None, True, DEC_BATCH, 1)

    (shift_p, scale_p, gate_p), (shift_s, scale_s, gate_s) = mods(1)
    w_b = wb_in[0].astype(BF16)
    b_b = bb_in[0].reshape(1, B_IN)
    wo_b = wb_o[0].astype(BF16)
    bo_b = bb_o[0].reshape(1, D_MODEL)
    gfin = g_final.reshape(1, D_MODEL)
    sinks = wb_sinks[0]

    q_p, k_p, v_p, z_p = _swa_proj(xp, shift_p, scale_p, w_b, b_b, False, tm_p, tiles_p)
    o_p = _swa_prompt(sinks, q_p, k_p, v_p)
    y_p = _outproj(True, o_p, z_p, wo_b, bo_b, xp, gate_p, gfin, False, tm_p, tiles_p)

    q_s, k_s, v_s, z_s = _swa_proj(xs, shift_s, scale_s, w_b, b_b, True, DEC_BATCH, 1)
    slopes = jnp.power(2.0, -8.0 * jnp.arange(1, B_HEADS + 1, dtype=F32) / B_HEADS)
    behind = (WINDOW - 1 - jnp.arange(WINDOW)).astype(F32)
    bias_s = jnp.tile(-slopes[:, None] * behind[None, :], (SWA_DEC_TILE, 1))
    sink_s = jnp.tile(jnp.broadcast_to(sinks[:, None], (B_HEADS, LANE)), (SWA_DEC_TILE, 1))
    o_s, kwin, vwin = _swa_decode(
        bias_s, sink_s, q_s.astype(F32).reshape(DEC_BATCH, B_HEADS, B_HEAD_DIM), k_s, v_s,
        state_swa_k[0].reshape(DEC_BATCH, WINDOW, B_KV_W),
        state_swa_v[0].reshape(DEC_BATCH, WINDOW, B_KV_W))
    y_s = _outproj(True, o_s.reshape(DEC_BATCH, B_WIDTH), z_s, wo_b, bo_b, xs, gate_s, gfin,
                   True, DEC_BATCH, 1)

    kv_shape = (1, BATCH, WINDOW, B_KV_HEADS, B_HEAD_DIM)
    win_shape = (1, DEC_BATCH, WINDOW, B_KV_HEADS, B_HEAD_DIM)
    return (y_p.reshape(BATCH, SEQ, D_MODEL),
            y_s.reshape(DEC_BATCH, 1, D_MODEL),
            rows_p.reshape(1, BATCH, SEQ, A_CACHE_W),
            rows_s.reshape(1, DEC_BATCH, 1, A_CACHE_W),
            k_p.reshape(BATCH, SEQ, B_KV_W)[:, SEQ - WINDOW:].reshape(kv_shape),
            v_p.reshape(BATCH, SEQ, B_KV_W)[:, SEQ - WINDOW:].reshape(kv_shape),
            kwin.reshape(win_shape),
            vwin.reshape(win_shape))
```

```python
import functools

import jax
import jax.numpy as jnp
from jax import lax
from jax.experimental import pallas as pl
from jax.experimental.pallas import tpu as pltpu

F32 = jnp.float32
BF16 = jnp.bfloat16

D_MODEL = 2048
BATCH = 2
SEQ = 4096
DEC_BATCH = 128
PAST_LEN = 16384
PAGE_SIZE = 128
N_PAGES = PAST_LEN // PAGE_SIZE

A_HEADS = 16
A_Q_LORA = 512
A_KV_LORA = 512
A_NOPE = 128
A_ROPE = 64
A_VDIM = 128
A_QK = A_NOPE + A_ROPE
A_WIDTH = A_HEADS * A_VDIM
A_CACHE_W = A_KV_LORA + A_ROPE
A_HEAD_PAD = 256
A_IN_EXT = A_Q_LORA + A_KV_LORA + 2 * A_ROPE + A_WIDTH
ROPE_THETA = 10000.0

B_HEADS = 32
B_KV_HEADS = 4
B_GROUP = 8
B_HEAD_DIM = 64
B_WIDTH = B_HEADS * B_HEAD_DIM
B_KV_W = B_KV_HEADS * B_HEAD_DIM
B_IN = 2 * B_WIDTH + 2 * B_KV_W
WINDOW = 128

NORM_EPS = 1e-6
NEG_INF = -1e30
A_SCALE = A_QK ** -0.5
B_SCALE = B_HEAD_DIM ** -0.5
LOG2_E = 1.4426950408889634

LANE = 128
VMEM_LIMIT = 56 * 1024 * 1024

_NT = (((1,), (1,)), ((), ()))


def _params(n_grid):
    return pltpu.CompilerParams(
        dimension_semantics=("arbitrary",) * n_grid, vmem_limit_bytes=VMEM_LIMIT)


def _resident(shape):
    nd = len(shape)
    return pl.BlockSpec(shape, lambda *_: (0,) * nd, pipeline_mode=pl.Buffered(1))


def _rms(x):
    return x * lax.rsqrt(jnp.mean(x * x, -1, keepdims=True) + NORM_EPS)


def _silu(x):
    return x * jax.nn.sigmoid(x)


def _dot(a, b):
    return jnp.dot(a, b, preferred_element_type=F32)


def _dot_nt(a, b):
    return lax.dot_general(a, b, _NT, preferred_element_type=F32)


def _ada_kernel(c_ref, w_ref, b_ref, o_ref):
    a = _silu(c_ref[...]).astype(BF16)
    o_ref[...] = _dot(a, w_ref[...].astype(BF16)) + b_ref[...]


def _ada_mod(c_all, w_ada, b_ada):
    depth, _, n = w_ada.shape
    m = c_all.shape[0]
    tn = 1024
    return pl.pallas_call(
        _ada_kernel,
        out_shape=jax.ShapeDtypeStruct((depth, m, n), F32),
        grid=(depth, n // tn),
        in_specs=[
            pl.BlockSpec((m, D_MODEL), lambda i, j: (0, 0)),
            pl.BlockSpec((None, D_MODEL, tn), lambda i, j: (i, 0, j)),
            pl.BlockSpec((None, 1, tn), lambda i, j: (i, 0, j)),
        ],
        out_specs=pl.BlockSpec((None, m, tn), lambda i, j: (i, 0, j)),
        compiler_params=_params(2),
        name="ada_mod",
    )(c_all, w_ada, b_ada.reshape(depth, 1, n))


def _mod_specs(per_token, tm, tiles_per_seq):
    if per_token:
        return pl.BlockSpec((tm, D_MODEL), lambda i: (i, 0))
    return pl.BlockSpec((None, 1, D_MODEL), lambda i: (i // tiles_per_seq, 0, 0))


def _rope_slab(slab, cc, ss):
    return slab * cc + pltpu.roll(slab, A_ROPE, 1) * ss


def _mla_proj_kernel(decode, x_ref, shift_ref, scale_ref, cc_ref, ss_ref, win_ref, gq_ref,
                     gkv_ref, wuq_ref, wa_ref, wb_ref, *outs):
    if decode:
        z_ref, rows_ref, qabs_ref, qpe_ref = outs
    else:
        z_ref, rows_ref, qcat_ref, kcat_ref, v_ref = outs
    h = (_rms(x_ref[...]) * (1.0 + scale_ref[...]) + shift_ref[...]).astype(BF16)
    o_q, o_kv, o_pe, o_z = 0, A_Q_LORA, A_Q_LORA + A_KV_LORA, A_Q_LORA + A_KV_LORA + 2 * A_ROPE
    q_lat = _dot(h, win_ref[:, o_q:o_kv])
    kv_lat = _dot(h, win_ref[:, o_kv:o_pe])
    pe2 = _dot(h, win_ref[:, o_pe:o_z])
    z_ref[...] = _dot(h, win_ref[:, o_z:])
    cc = cc_ref[...]
    ss = ss_ref[...]
    kv = _rms(kv_lat) * gkv_ref[...]
    kpe = _rope_slab(pe2, cc, ss)
    rows_ref[:, :A_KV_LORA] = kv
    rows_ref[:, A_KV_LORA:] = kpe[:, :A_ROPE]
    qn = (_rms(q_lat) * gq_ref[...]).astype(BF16)
    q_scale = A_SCALE if decode else A_SCALE * LOG2_E
    if not decode:
        kvb = kv.astype(BF16)
        kpe_b = kpe.astype(BF16)
    for hd in range(A_HEADS):
        lo = hd * A_HEAD_PAD
        qh = _dot(qn, wuq_ref[:, lo:lo + A_HEAD_PAD])
        nope = qh[:, :A_NOPE]
        qpe = (_rope_slab(qh[:, A_NOPE:], cc, ss) * q_scale).astype(BF16)
        if decode:
            qabs = _dot(nope.astype(BF16), wa_ref[hd])
            qabs_ref[:, hd * A_KV_LORA:(hd + 1) * A_KV_LORA] = (qabs * A_SCALE).astype(BF16)
            qpe_ref[:, hd * LANE:(hd + 1) * LANE] = qpe
        else:
            qcat_ref[:, lo:lo + A_NOPE] = (nope * q_scale).astype(BF16)
            qcat_ref[:, lo + A_NOPE:lo + A_HEAD_PAD] = qpe
            knope = _dot(kvb, wa_ref[:, hd * A_NOPE:(hd + 1) * A_NOPE])
            kcat_ref[:, lo:lo + A_NOPE] = knope.astype(BF16)
            kcat_ref[:, lo + A_NOPE:lo + A_HEAD_PAD] = kpe_b
            wv_t = wb_ref[hd * A_VDIM:(hd + 1) * A_VDIM, :]
            v_ref[hd] = _dot_nt(wv_t, kvb).astype(BF16)


def _mla_proj(decode, x, shift, scale, cc, ss, win, gq, gkv, wuq, wa, wb, tm, tiles_per_seq):
    m = x.shape[0]
    mod_spec = _mod_specs(decode, tm, tiles_per_seq)
    tab_spec = pl.BlockSpec((tm, LANE), lambda i: (i % tiles_per_seq, 0))
    row = lambda n: pl.BlockSpec((tm, n), lambda i: (i, 0))
    out_shape = [jax.ShapeDtypeStruct((m, A_WIDTH), F32),
                 jax.ShapeDtypeStruct((m, A_CACHE_W), F32)]
    out_specs = [row(A_WIDTH), row(A_CACHE_W)]
    if decode:
        out_shape += [jax.ShapeDtypeStruct((m, A_HEADS * A_KV_LORA), BF16),
                      jax.ShapeDtypeStruct((m, A_HEADS * LANE), BF16)]
        out_specs += [row(A_HEADS * A_KV_LORA), row(A_HEADS * LANE)]
    else:
        out_shape += [jax.ShapeDtypeStruct((m, A_HEADS * A_HEAD_PAD), BF16),
                      jax.ShapeDtypeStruct((m, A_HEADS * A_HEAD_PAD), BF16),
                      jax.ShapeDtypeStruct(
                          (BATCH, A_HEADS, SEQ // FLASH_TK, A_VDIM, FLASH_TK), BF16)]
        per_blk = FLASH_TK // tm

        def v_map(i):
            t = i % tiles_per_seq
            return (i // tiles_per_seq, 0, t // per_blk, 0, t % per_blk)

        out_specs += [row(A_HEADS * A_HEAD_PAD), row(A_HEADS * A_HEAD_PAD),
                      pl.BlockSpec((None, A_HEADS, None, A_VDIM, tm), v_map)]
    return pl.pallas_call(
        functools.partial(_mla_proj_kernel, decode),
        out_shape=out_shape,
        grid=(m // tm,),
        in_specs=[row(D_MODEL), mod_spec, mod_spec, tab_spec, tab_spec,
                  _resident(win.shape), _resident(gq.shape), _resident(gkv.shape),
                  _resident(wuq.shape), _resident(wa.shape), _resident(wb.shape)],
        out_specs=out_specs,
        compiler_params=_params(1),
        name="mla_proj_decode" if decode else "mla_proj_prompt",
    )(x, shift, scale, cc, ss, win, gq, gkv, wuq, wa, wb)


FLASH_TQ = 1024
FLASH_TK = 512
FLASH_CHAIN = 256
FLASH_DEN_ROWS = 16


def _flash_kernel(q_ref, k_ref, vt_ref, o_ref, m_ref, acc_ref, sa_ref, sb_ref):
    tq, tk, cw = FLASH_TQ, FLASH_TK, FLASH_CHAIN
    n_chain = tq // cw
    qi = pl.program_id(2)
    m_ref[...] = jnp.full(m_ref.shape, NEG_INF, F32)
    acc_ref[...] = jnp.zeros(acc_ref.shape, F32)
    ones = jnp.ones((FLASH_DEN_ROWS, tk), BF16)

    def update(c, s, vt_blk):
        m_prev = m_ref[c]
        m_new = jnp.maximum(m_prev, jnp.max(s, 0, keepdims=True))
        alpha = jnp.exp2(m_prev - m_new)
        p = jnp.exp2(s - m_new).astype(BF16)
        acc_ref[c] = alpha * acc_ref[c] + _dot(vt_blk, p)
        m_ref[c] = m_new

    def visible(key_off):
        return [c for c in range(n_chain) if key_off is None or key_off <= (c + 1) * cw - 1]

    def score(ki, s_ref, key_off):
        start = pl.multiple_of(ki * tk, tk)
        k_blk = k_ref[pl.ds(start, tk), :]
        for c in visible(key_off):
            s_ref[c] = _dot_nt(k_blk, q_ref[c * cw:(c + 1) * cw, :])

    def consume(ki, s_ref, key_off):
        vt_blk = jnp.concatenate([vt_ref[ki], ones], 0)
        for c in visible(key_off):
            s = s_ref[c]
            if key_off is not None and key_off + tk - 1 > c * cw:
                key = lax.broadcasted_iota(jnp.int32, (tk, cw), 0) + key_off
                qry = lax.broadcasted_iota(jnp.int32, (tk, cw), 1) + c * cw
                s = jnp.where(key <= qry, s, NEG_INF)
            update(c, s, vt_blk)

    assert tq == 2 * tk
    score(0, sa_ref, None)

    def body(j, carry):
        score(2 * j + 1, sb_ref, None)
        consume(2 * j, sa_ref, None)
        score(2 * j + 2, sa_ref, None)
        consume(2 * j + 1, sb_ref, None)
        return carry

    lax.fori_loop(0, qi, body, 0)
    score(2 * qi + 1, sb_ref, tk)
    consume(2 * qi, sa_ref, 0)
    consume(2 * qi + 1, sb_ref, tk)
    for c in range(n_chain):
        den = acc_ref[c, A_VDIM:A_VDIM + 1, :]
        o_t = acc_ref[c, :A_VDIM, :] * (1.0 / den)
        o_ref[c * cw:(c + 1) * cw, :] = o_t.T.astype(o_ref.dtype)


def _flash(qcat, kcat, vt):
    tq, tk, cw = FLASH_TQ, FLASH_TK, FLASH_CHAIN
    nq, nk, n_chain = SEQ // tq, SEQ // tk, tq // cw
    return pl.pallas_call(
        _flash_kernel,
        out_shape=jax.ShapeDtypeStruct((BATCH * SEQ, A_WIDTH), BF16),
        grid=(BATCH, A_HEADS, nq),
        in_specs=[
            pl.BlockSpec((tq, A_HEAD_PAD), lambda b, h, i: (b * nq + i, h)),
            pl.BlockSpec((SEQ, A_HEAD_PAD), lambda b, h, i: (b, h)),
            pl.BlockSpec((None, None, nk, A_VDIM, tk), lambda b, h, i: (b, h, 0, 0, 0)),
        ],
        out_specs=pl.BlockSpec((tq, A_VDIM), lambda b, h, i: (b * nq + i, h)),
        scratch_shapes=[pltpu.VMEM((n_chain, 1, cw), F32),
                        pltpu.VMEM((n_chain, A_VDIM + FLASH_DEN_ROWS, cw), F32),
                        pltpu.VMEM((n_chain, tk, cw), F32), pltpu.VMEM((n_chain, tk, cw), F32)],
        compiler_params=_params(3),
        name="mla_flash",
    )(qcat, kcat, vt)


DEC_PAGES_PER_STEP = 32
DEC_GROUPS = 4


def _mla_decode_kernel(pt_ref, qabs_ref, qpe_ref, row_ref, cache_ref, o_ref,
                       m_ref, l_ref, acc_ref, buf_ref, sem_ref):
    n = DEC_PAGES_PER_STEP
    b = pl.program_id(0)
    j = pl.program_id(1)
    n_chunk = pl.num_programs(1)
    step = b * n_chunk + j
    slot = lax.rem(step, 2)

    def page_copy(seq, chunk, i, half):
        page = pt_ref[seq, chunk * n + i]
        return pltpu.make_async_copy(cache_ref.at[0, page], buf_ref.at[half, i], sem_ref.at[half])

    @pl.when(step == 0)
    def _():
        for i in range(n):
            page_copy(0, 0, i, 0).start()

    wraps = j + 1 == n_chunk
    next_seq = jnp.where(wraps, b + 1, b)
    next_chunk = jnp.where(wraps, 0, j + 1)

    @pl.when(next_seq < pl.num_programs(0))
    def _():
        for i in range(n):
            page_copy(next_seq, next_chunk, i, 1 - slot).start()

    for i in range(n):
        page_copy(b, j, i, slot).wait()
    page_refs = [buf_ref.at[slot, i] for i in range(n)]

    @pl.when(j == 0)
    def _():
        m_ref[...] = jnp.full(m_ref.shape, NEG_INF, F32)
        l_ref[...] = jnp.zeros(l_ref.shape, F32)
        acc_ref[...] = jnp.zeros(acc_ref.shape, F32)

    qa = qabs_ref[...]
    qp = qpe_ref[:, :A_ROPE]
    per_group = n // DEC_GROUPS

    def score(g):
        refs = page_refs[g * per_group:(g + 1) * per_group]
        pages = [r[...].astype(BF16) for r in refs]
        kv_t = jnp.concatenate([pg[:A_KV_LORA] for pg in pages], 1)
        pe_t = jnp.concatenate([pg[A_KV_LORA:] for pg in pages], 1)
        return _dot(qa, kv_t) + _dot(qp, pe_t), kv_t

    def update(s, kv_t):
        m_prev = m_ref[...]
        m_new = jnp.maximum(m_prev, jnp.max(s, -1, keepdims=True))
        alpha = jnp.exp(m_prev - m_new)
        p = jnp.exp(s - m_new)
        l_ref[...] = alpha * l_ref[...] + jnp.sum(p, -1, keepdims=True)
        acc_ref[...] = alpha * acc_ref[...] + _dot_nt(p.astype(BF16), kv_t)
        m_ref[...] = m_new

    pending = score(0)
    for g in range(1, DEC_GROUPS):
        ahead = score(g)
        update(*pending)
        pending = ahead
    update(*pending)

    @pl.when(j == pl.num_programs(1) - 1)
    def _():
        rb = row_ref[...].astype(BF16).astype(F32)
        s_new = (jnp.sum(qa.astype(F32) * rb[:, :A_KV_LORA], -1, keepdims=True)
                 + jnp.sum(qp.astype(F32) * rb[:, A_KV_LORA:], -1, keepdims=True))
        m_old = m_ref[...]
        m_fin = jnp.maximum(m_old, s_new)
        corr = jnp.exp(m_old - m_fin)
        p_new = jnp.exp(s_new - m_fin)
        l_fin = l_ref[...] * corr + p_new
        acc_fin = acc_ref[...] * corr + p_new.astype(BF16).astype(F32) * rb[:, :A_KV_LORA]
        o_ref[...] = acc_fin / l_fin


def _mla_decode(page_table, qabs, qpe, rows, cache):
    n = DEC_PAGES_PER_STEP
    grid_spec = pltpu.PrefetchScalarGridSpec(
        num_scalar_prefetch=1,
        grid=(DEC_BATCH, N_PAGES // n),
        in_specs=[
            pl.BlockSpec((None, A_HEADS, A_KV_LORA), lambda b, j, pt: (b, 0, 0)),
            pl.BlockSpec((None, A_HEADS, LANE), lambda b, j, pt: (b, 0, 0)),
            pl.BlockSpec((None, 1, A_CACHE_W), lambda b, j, pt: (b, 0, 0)),
            pl.BlockSpec(memory_space=pl.ANY),
        ],
        out_specs=pl.BlockSpec((None, A_HEADS, A_KV_LORA), lambda b, j, pt: (b, 0, 0)),
        scratch_shapes=[pltpu.VMEM((A_HEADS, 1), F32), pltpu.VMEM((A_HEADS, 1), F32),
                        pltpu.VMEM((A_HEADS, A_KV_LORA), F32),
                        pltpu.VMEM((2, n, A_CACHE_W, PAGE_SIZE), F32),
                        pltpu.SemaphoreType.DMA((2,))],
    )
    return pl.pallas_call(
        _mla_decode_kernel,
        out_shape=jax.ShapeDtypeStruct((DEC_BATCH, A_HEADS, A_KV_LORA), F32),
        grid_spec=grid_spec,
        compiler_params=_params(2),
        name="mla_decode",
    )(page_table, qabs, qpe, rows, cache)


def _uv_kernel(o_ref, w_ref, out_ref):
    out_ref[...] = _dot(o_ref[...].astype(BF16), w_ref[...]).astype(out_ref.dtype)


def _mla_decode_uv(o_lat2d, wuv_heads):
    m = o_lat2d.shape[0]
    return pl.pallas_call(
        _uv_kernel,
        out_shape=jax.ShapeDtypeStruct((m, A_WIDTH), BF16),
        grid=(A_HEADS,),
        in_specs=[pl.BlockSpec((m, A_KV_LORA), lambda h: (0, h)),
                  pl.BlockSpec((None, A_KV_LORA, A_VDIM), lambda h: (h, 0, 0))],
        out_specs=pl.BlockSpec((m, A_VDIM), lambda h: (0, h)),
        compiler_params=_params(1),
        name="mla_decode_uv",
    )(o_lat2d, wuv_heads)


def _outproj_kernel(final, o_ref, z_ref, w_ref, b_ref, x_ref, gate_ref, *rest):
    g = (o_ref[...].astype(F32) * _silu(z_ref[...])).astype(BF16)
    y = _dot(g, w_ref[...]) + b_ref[...]
    x_new = x_ref[...] + gate_ref[...] * y
    if final:
        gfin_ref, out_ref = rest
        out_ref[...] = _rms(x_new) * gfin_ref[...]
    else:
        (out_ref,) = rest
        out_ref[...] = x_new


def _outproj(final, o, z, w, b, x, gate, gfin, per_token, tm, tiles_per_seq):
    m = x.shape[0]
    row = lambda n: pl.BlockSpec((tm, n), lambda i: (i, 0))
    in_specs = [row(o.shape[1]), row(z.shape[1]), _resident(w.shape), _resident(b.shape),
                row(D_MODEL), _mod_specs(per_token, tm, tiles_per_seq)]
    args = [o, z, w, b, x, gate]
    if final:
        in_specs.append(_resident(gfin.shape))
        args.append(gfin)
    return pl.pallas_call(
        functools.partial(_outproj_kernel, final),
        out_shape=jax.ShapeDtypeStruct((m, D_MODEL), F32),
        grid=(m // tm,),
        in_specs=in_specs,
        out_specs=row(D_MODEL),
        compiler_params=_params(1),
        name="outproj_final" if final else "outproj",
    )(*args)


def _swa_proj_kernel(q_scale, x_ref, shift_ref, scale_ref, w_ref, b_ref, q_ref, k_ref, v_ref, z_ref):
    h = (_rms(x_ref[...]) * (1.0 + scale_ref[...]) + shift_ref[...]).astype(BF16)
    o_k, o_v, o_z = B_WIDTH, B_WIDTH + B_KV_W, B_WIDTH + 2 * B_KV_W
    q = _dot(h, w_ref[:, :o_k]) + b_ref[:, :o_k]
    q_ref[...] = (q * q_scale).astype(BF16)
    k_ref[...] = _dot(h, w_ref[:, o_k:o_v]) + b_ref[:, o_k:o_v]
    v_ref[...] = _dot(h, w_ref[:, o_v:o_z]) + b_ref[:, o_v:o_z]
    z_ref[...] = _dot(h, w_ref[:, o_z:]) + b_ref[:, o_z:]


def _swa_proj(q_scale, x, shift, scale, w, b, per_token, tm, tiles_per_seq):
    m = x.shape[0]
    row = lambda n: pl.BlockSpec((tm, n), lambda i: (i, 0))
    mod_spec = _mod_specs(per_token, tm, tiles_per_seq)
    return pl.pallas_call(
        functools.partial(_swa_proj_kernel, q_scale),
        out_shape=[jax.ShapeDtypeStruct((m, B_WIDTH), BF16),
                   jax.ShapeDtypeStruct((m, B_KV_W), F32),
                   jax.ShapeDtypeStruct((m, B_KV_W), F32),
                   jax.ShapeDtypeStruct((m, B_WIDTH), F32)],
        grid=(m // tm,),
        in_specs=[row(D_MODEL), mod_spec, mod_spec, _resident(w.shape), _resident(b.shape)],
        out_specs=[row(B_WIDTH), row(B_KV_W), row(B_KV_W), row(B_WIDTH)],
        compiler_params=_params(1),
        name="swa_proj",
    )(x, shift, scale, w, b)


def _block_diag2(t):
    zero = jnp.zeros_like(t)
    return jnp.concatenate([jnp.concatenate([t, zero], 1), jnp.concatenate([zero, t], 1)], 0)


def _swa_prompt_kernel(sink_ref, q_ref, kc_ref, kp_ref, vc_ref, vp_ref, bias_ref, o_ref):
    w = WINDOW
    kband = jnp.concatenate([kp_ref[...], kc_ref[...]], 0).astype(BF16)
    vband = jnp.concatenate([vp_ref[...], vc_ref[...]], 0).astype(BF16)
    pair_w = 2 * B_HEAD_DIM
    n_pair = B_HEADS // 2
    pairs_per_kv = B_GROUP // 2
    one = jnp.ones((2 * w, B_HEAD_DIM), BF16)
    ones_bd = _block_diag2(one)
    first_head = lax.broadcasted_iota(jnp.int32, (w, pair_w), 1) < B_HEAD_DIM
    scores, vbs = [], []
    for kvh in range(B_KV_HEADS):
        kb = _block_diag2(kband[:, kvh * B_HEAD_DIM:(kvh + 1) * B_HEAD_DIM])
        vb = _block_diag2(vband[:, kvh * B_HEAD_DIM:(kvh + 1) * B_HEAD_DIM])
        vbs.append(jnp.concatenate([vb, ones_bd], 1))
        for gp in range(pairs_per_kv):
            lo = (kvh * pairs_per_kv + gp) * pair_w
            scores.append(_dot_nt(q_ref[:, lo:lo + pair_w], kb))
    probs, sink_terms = [], []
    for idx in range(n_pair):
        both, sink_p = [], []
        for t in range(2):
            head = 2 * idx + t
            s = scores[idx][:, t * 2 * w:(t + 1) * 2 * w] + bias_ref[head]
            sk = sink_ref[head]
            m = jnp.maximum(jnp.max(s, -1, keepdims=True), sk)
            both.append(jnp.exp2(s - m).astype(BF16))
            sink_p.append(jnp.exp2(sk - m))
        probs.append(jnp.concatenate(both, 1))
        sink_terms.append(jnp.where(first_head, sink_p[0], sink_p[1]))
    for idx in range(n_pair):
        o_den = _dot(probs[idx], vbs[idx // pairs_per_kv])
        den = o_den[:, pair_w:] + sink_terms[idx]
        o_ref[:, idx * pair_w:(idx + 1) * pair_w] = (
            o_den[:, :pair_w] * (1.0 / den)).astype(o_ref.dtype)


def _swa_prompt_bias():
    i = jnp.arange(WINDOW)[:, None]
    j = jnp.arange(2 * WINDOW)[None, :]
    dist = i - j + WINDOW
    valid = (dist >= 0) & (dist < WINDOW)
    slopes = jnp.power(2.0, -8.0 * jnp.arange(1, B_HEADS + 1, dtype=F32) / B_HEADS)
    alibi = -slopes[:, None, None] * dist.astype(F32)
    alibi = alibi * LOG2_E
    later = jnp.where(valid[None], alibi, NEG_INF)
    first = jnp.where((valid & (j >= WINDOW))[None], alibi, NEG_INF)
    return jnp.stack([first, later])


def _swa_prompt(sinks, q, k, v):
    nb = SEQ // WINDOW
    cur = lambda b, n, s: (b * nb + n, 0)
    prev = lambda b, n, s: (b * nb + jnp.maximum(n - 1, 0), 0)
    grid_spec = pltpu.PrefetchScalarGridSpec(
        num_scalar_prefetch=1,
        grid=(BATCH, nb),
        in_specs=[pl.BlockSpec((WINDOW, B_WIDTH), cur),
                  pl.BlockSpec((WINDOW, B_KV_W), cur), pl.BlockSpec((WINDOW, B_KV_W), prev),
                  pl.BlockSpec((WINDOW, B_KV_W), cur), pl.BlockSpec((WINDOW, B_KV_W), prev),
                  pl.BlockSpec((None, B_HEADS, WINDOW, 2 * WINDOW),
                               lambda b, n, s: (jnp.minimum(n, 1), 0, 0, 0))],
        out_specs=pl.BlockSpec((WINDOW, B_WIDTH), cur),
    )
    return pl.pallas_call(
        _swa_prompt_kernel,
        out_shape=jax.ShapeDtypeStruct((BATCH * SEQ, B_WIDTH), BF16),
        grid_spec=grid_spec,
        compiler_params=_params(2),
        name="swa_prompt",
    )(sinks, q, k, k, v, v, _swa_prompt_bias())


SWA_DEC_TILE = 16


def _swa_decode_kernel(bias_ref, sink_ref, q_ref, knew_ref, vnew_ref, kbuf_ref, vbuf_ref,
                       o_ref, kout_ref, vout_ref):
    w = WINDOW
    last = lax.broadcasted_iota(jnp.int32, (w, B_KV_W), 0) == w - 1
    group_bits = B_GROUP.bit_length() - 1
    dim_bits = B_HEAD_DIM.bit_length() - 1
    own = (jnp.right_shift(lax.broadcasted_iota(jnp.int32, (B_HEADS, B_KV_W), 0), group_bits)
           == jnp.right_shift(lax.broadcasted_iota(jnp.int32, (B_HEADS, B_KV_W), 1), dim_bits))
    scores, vals = [], []
    for i in range(SWA_DEC_TILE):
        kb = jnp.where(last, knew_ref[i:i + 1, :], pltpu.roll(kbuf_ref[i], w - 1, 0))
        vb = jnp.where(last, vnew_ref[i:i + 1, :], pltpu.roll(vbuf_ref[i], w - 1, 0))
        kout_ref[i] = kb
        vout_ref[i] = vb
        q_wide = jnp.concatenate([q_ref[i]] * B_KV_HEADS, 1)
        q_bd = jnp.where(own, q_wide, 0.0).astype(BF16)
        scores.append(_dot_nt(q_bd, kb.astype(BF16)))
        vals.append(vb.astype(BF16))
    s = jnp.concatenate(scores, 0) + bias_ref[...]
    sk = sink_ref[...]
    m = jnp.maximum(jnp.max(s, -1, keepdims=True), sk)
    p = jnp.exp(s - m)
    den = jnp.sum(p, -1, keepdims=True) + jnp.exp(sk - m)
    pn = (p * (1.0 / den)).astype(BF16)
    kv_of_row = jnp.right_shift(
        lax.broadcasted_iota(jnp.int32, (B_HEADS, B_HEAD_DIM), 0), group_bits)
    for i in range(SWA_DEC_TILE):
        o_all = _dot(pn[i * B_HEADS:(i + 1) * B_HEADS], vals[i])
        o = jnp.zeros((B_HEADS, B_HEAD_DIM), F32)
        for kvh in range(B_KV_HEADS):
            part = o_all[:, kvh * B_HEAD_DIM:(kvh + 1) * B_HEAD_DIM]
            o = o + jnp.where(kv_of_row == kvh, part, 0.0)
        o_ref[i] = o


def _swa_decode(bias, sinks, q, knew, vnew, kbuf, vbuf):
    t = SWA_DEC_TILE
    head_tab = pl.BlockSpec((t * B_HEADS, LANE), lambda i: (0, 0))
    qo = pl.BlockSpec((t, B_HEADS, B_HEAD_DIM), lambda i: (i, 0, 0))
    new = pl.BlockSpec((t, B_KV_W), lambda i: (i, 0))
    buf = pl.BlockSpec((t, WINDOW, B_KV_W), lambda i: (i, 0, 0))
    return pl.pallas_call(
        _swa_decode_kernel,
        out_shape=[jax.ShapeDtypeStruct((DEC_BATCH, B_HEADS, B_HEAD_DIM), F32),
                   jax.ShapeDtypeStruct(kbuf.shape, F32),
                   jax.ShapeDtypeStruct(vbuf.shape, F32)],
        grid=(DEC_BATCH // t,),
        in_specs=[head_tab, head_tab, qo, new, new, buf, buf],
        out_specs=[qo, buf, buf],
        compiler_params=_params(1),
        name="swa_decode",
    )(bias, sinks, q, knew, vnew, kbuf, vbuf)


def _rope_tables(pos):
    inv = jnp.power(ROPE_THETA, -jnp.arange(0, A_ROPE, 2, dtype=F32) / A_ROPE)
    ang = pos.astype(F32)[:, None] * inv[None, :]
    zero = jnp.zeros((pos.shape[0], LANE - A_ROPE), F32)
    cos, sin = jnp.cos(ang), jnp.sin(ang)
    return jnp.concatenate([cos, cos, zero], -1), jnp.concatenate([sin, sin, zero], -1)


def _with_rot(w):
    half = A_ROPE // 2
    return jnp.concatenate([w, -w[..., half:], w[..., :half]], -1)


def kernel(x_prompt, x_sample, cache_mla, state_swa_k, state_swa_v, page_table, c_prompt, c_sample,
           w_ada, b_ada, wa_in, ga_q, ga_kv, wa_uq, wa_uk, wa_uv, wa_o,
           wb_in, bb_in, wb_sinks, wb_o, bb_o, g_final):
    n_p = BATCH * SEQ
    xp = x_prompt.reshape(n_p, D_MODEL)
    xs = x_sample.reshape(DEC_BATCH, D_MODEL)

    pad = jnp.zeros((8 - BATCH, D_MODEL), F32)
    mod = _ada_mod(jnp.concatenate([c_sample, c_prompt, pad], 0), w_ada, b_ada)

    def mods(i):
        parts = jnp.split(mod[i], 3, -1)
        sample = [p[:DEC_BATCH] for p in parts]
        prompt = [p[DEC_BATCH:DEC_BATCH + BATCH].reshape(BATCH, 1, D_MODEL) for p in parts]
        return prompt, sample

    (shift_p, scale_p, gate_p), (shift_s, scale_s, gate_s) = mods(0)
    o_pe = A_Q_LORA + A_KV_LORA
    w_in = wa_in[0].astype(BF16)
    win = jnp.concatenate([w_in[:, :o_pe], _with_rot(w_in[:, o_pe:o_pe + A_ROPE]),
                           w_in[:, o_pe + A_ROPE:]], -1)
    wuq3 = wa_uq[0].astype(BF16).reshape(A_Q_LORA, A_HEADS, A_QK)
    wuq = jnp.concatenate([wuq3[..., :A_NOPE], _with_rot(wuq3[..., A_NOPE:])], -1).reshape(
        A_Q_LORA, A_HEADS * A_HEAD_PAD)
    wuk_b = wa_uk[0].astype(BF16)
    wuv_b = wa_uv[0].astype(BF16)
    wuk = wuk_b.reshape(A_KV_LORA, A_HEADS * A_NOPE)
    wuv_t = wuv_b.reshape(A_KV_LORA, A_WIDTH).T
    wuk_t = wuk_b.transpose(1, 2, 0)
    wuv_h = wuv_b.transpose(1, 0, 2)
    gq = ga_q[0].reshape(1, A_Q_LORA)
    gkv = ga_kv[0].reshape(1, A_KV_LORA)
    wo_a = wa_o[0].astype(BF16)
    zero_bias = jnp.zeros((1, D_MODEL), F32)

    tm_p = 256
    tiles_p = SEQ // tm_p
    tm_w = 512
    tiles_w = SEQ // tm_w
    cc_p, ss_p = _rope_tables(jnp.arange(SEQ))
    z_p, rows_p, qcat, kcat, v_p = _mla_proj(
        False, xp, shift_p, scale_p, cc_p, ss_p, win, gq, gkv, wuq, wuk, wuv_t, tm_p, tiles_p)
    o_p = _flash(qcat, kcat, v_p)
    xp = _outproj(False, o_p, z_p, wo_a, zero_bias, xp, gate_p, None, False, tm_w, tiles_w)

    cc_s, ss_s = _rope_tables(jnp.full((DEC_BATCH,), PAST_LEN))
    z_s, rows_s, qabs, qpe = _mla_proj(
        True, xs, shift_s, scale_s, cc_s, ss_s, win, gq, gkv, wuq, wuk_t, wuv_t, DEC_BATCH, 1)
    o_lat = _mla_decode(page_table,
                        qabs.reshape(DEC_BATCH, A_HEADS, A_KV_LORA),
                        qpe.reshape(DEC_BATCH, A_HEADS, LANE),
                        rows_s.reshape(DEC_BATCH, 1, A_CACHE_W), jnp.swapaxes(cache_mla, 2, 3))
    o_s = _mla_decode_uv(o_lat.reshape(DEC_BATCH, A_HEADS * A_KV_LORA), wuv_h)
    xs = _outproj(False, o_s, z_s, wo_a, zero_bias, xs, gate_s, None, True, DEC_BATCH, 1)

    (shift_p, scale_p, gate_p), (shift_s, scale_s, gate_s) = mods(1)
    w_b = wb_in[0].astype(BF16)
    b_b = bb_in[0].reshape(1, B_IN)
    wo_b = wb_o[0].astype(BF16)
    bo_b = bb_o[0].reshape(1, D_MODEL)
    gfin = g_final.reshape(1, D_MODEL)
    sinks = wb_sinks[0]

    q_p, k_p, v_p, z_p = _swa_proj(B_SCALE * LOG2_E, xp, shift_p, scale_p, w_b, b_b, False,
                                   tm_w, tiles_w)
    o_p = _swa_prompt(sinks * LOG2_E, q_p, k_p, v_p)
    y_p = _outproj(True, o_p, z_p, wo_b, bo_b, xp, gate_p, gfin, False, tm_w, tiles_w)

    q_s, k_s, v_s, z_s = _swa_proj(B_SCALE, xs, shift_s, scale_s, w_b, b_b, True, DEC_BATCH, 1)
    slopes = jnp.power(2.0, -8.0 * jnp.arange(1, B_HEADS + 1, dtype=F32) / B_HEADS)
    behind = (WINDOW - 1 - jnp.arange(WINDOW)).astype(F32)
    bias_s = jnp.tile(-slopes[:, None] * behind[None, :], (SWA_DEC_TILE, 1))
    sink_s = jnp.tile(jnp.broadcast_to(sinks[:, None], (B_HEADS, LANE)), (SWA_DEC_TILE, 1))
    o_s, kwin, vwin = _swa_decode(
        bias_s, sink_s, q_s.astype(F32).reshape(DEC_BATCH, B_HEADS, B_HEAD_DIM), k_s, v_s,
        state_swa_k[0].reshape(DEC_BATCH, WINDOW, B_KV_W),
        state_swa_v[0].reshape(DEC_BATCH, WINDOW, B_KV_W))
    y_s = _outproj(True, o_s.reshape(DEC_BATCH, B_WIDTH), z_s, wo_b, bo_b, xs, gate_s, gfin,
                   True, DEC_BATCH, 1)

    kv_shape = (1, BATCH, WINDOW, B_KV_HEADS, B_HEAD_DIM)
    win_shape = (1, DEC_BATCH, WINDOW, B_KV_HEADS, B_HEAD_DIM)
    return (y_p.reshape(BATCH, SEQ, D_MODEL),
            y_s.reshape(DEC_BATCH, 1, D_MODEL),
            rows_p.reshape(1, BATCH, SEQ, A_CACHE_W),
            rows_s.reshape(1, DEC_BATCH, 1, A_CACHE_W),
            k_p.reshape(BATCH, SEQ, B_KV_W)[:, SEQ - WINDOW:].reshape(kv_shape),
            v_p.reshape(BATCH, SEQ, B_KV_W)[:, SEQ - WINDOW:].reshape(kv_shape),
            kwin.reshape(win_shape),
            vwin.reshape(win_shape))
```

```python
import functools

import jax
import jax.numpy as jnp
from jax import lax
from jax.experimental import pallas as pl
from jax.experimental.pallas import tpu as pltpu

F32 = jnp.float32
BF16 = jnp.bfloat16

D_MODEL = 2048
BATCH = 2
SEQ = 4096
DEC_BATCH = 128
PAST_LEN = 16384
PAGE_SIZE = 128
N_PAGES = PAST_LEN // PAGE_SIZE

A_HEADS = 16
A_Q_LORA = 512
A_KV_LORA = 512
A_NOPE = 128
A_ROPE = 64
A_VDIM = 128
A_QK = A_NOPE + A_ROPE
A_WIDTH = A_HEADS * A_VDIM
A_CACHE_W = A_KV_LORA + A_ROPE
A_HEAD_PAD = 256
A_IN_EXT = A_Q_LORA + A_KV_LORA + 2 * A_ROPE + A_WIDTH
ROPE_THETA = 10000.0

B_HEADS = 32
B_KV_HEADS = 4
B_GROUP = 8
B_HEAD_DIM = 64
B_WIDTH = B_HEADS * B_HEAD_DIM
B_KV_W = B_KV_HEADS * B_HEAD_DIM
B_IN = 2 * B_WIDTH + 2 * B_KV_W
WINDOW = 128

NORM_EPS = 1e-6
NEG_INF = -1e30
A_SCALE = A_QK ** -0.5
B_SCALE = B_HEAD_DIM ** -0.5
LOG2_E = 1.4426950408889634

LANE = 128
VMEM_LIMIT = 56 * 1024 * 1024

_NT = (((1,), (1,)), ((), ()))


def _params(n_grid):
    return pltpu.CompilerParams(
        dimension_semantics=("arbitrary",) * n_grid, vmem_limit_bytes=VMEM_LIMIT)


def _resident(shape):
    nd = len(shape)
    return pl.BlockSpec(shape, lambda *_: (0,) * nd, pipeline_mode=pl.Buffered(1))


def _rms(x):
    return x * lax.rsqrt(jnp.mean(x * x, -1, keepdims=True) + NORM_EPS)


def _silu(x):
    return x * jax.nn.sigmoid(x)


def _dot(a, b):
    return jnp.dot(a, b, preferred_element_type=F32)


def _dot_nt(a, b):
    return lax.dot_general(a, b, _NT, preferred_element_type=F32)


def _ada_kernel(c_ref, w_ref, b_ref, o_ref):
    a = _silu(c_ref[...]).astype(BF16)
    o_ref[...] = _dot(a, w_ref[...].astype(BF16)) + b_ref[...]


def _ada_mod(c_all, w_ada, b_ada):
    depth, _, n = w_ada.shape
    m = c_all.shape[0]
    tn = 1024
    return pl.pallas_call(
        _ada_kernel,
        out_shape=jax.ShapeDtypeStruct((depth, m, n), F32),
        grid=(depth, n // tn),
        in_specs=[
            pl.BlockSpec((m, D_MODEL), lambda i, j: (0, 0)),
            pl.BlockSpec((None, D_MODEL, tn), lambda i, j: (i, 0, j)),
            pl.BlockSpec((None, 1, tn), lambda i, j: (i, 0, j)),
        ],
        out_specs=pl.BlockSpec((None, m, tn), lambda i, j: (i, 0, j)),
        compiler_params=_params(2),
        name="ada_mod",
    )(c_all, w_ada, b_ada.reshape(depth, 1, n))


def _mod_specs(per_token, tm, tiles_per_seq):
    if per_token:
        return pl.BlockSpec((tm, D_MODEL), lambda i: (i, 0))
    return pl.BlockSpec((None, 1, D_MODEL), lambda i: (i // tiles_per_seq, 0, 0))


def _rope_slab(slab, cc, ss):
    return slab * cc + pltpu.roll(slab, A_ROPE, 1) * ss


def _mla_proj_kernel(decode, x_ref, shift_ref, scale_ref, cc_ref, ss_ref, win_ref, gq_ref,
                     gkv_ref, wuq_ref, wa_ref, wb_ref, *outs):
    if decode:
        z_ref, rows_ref, qabs_ref, qpe_ref = outs
    else:
        z_ref, rows_ref, qcat_ref, kcat_ref, v_ref = outs
    h = (_rms(x_ref[...]) * (1.0 + scale_ref[...]) + shift_ref[...]).astype(BF16)
    o_q, o_kv, o_pe, o_z = 0, A_Q_LORA, A_Q_LORA + A_KV_LORA, A_Q_LORA + A_KV_LORA + 2 * A_ROPE
    q_lat = _dot(h, win_ref[:, o_q:o_kv])
    kv_lat = _dot(h, win_ref[:, o_kv:o_pe])
    pe2 = _dot(h, win_ref[:, o_pe:o_z])
    z_ref[...] = _dot(h, win_ref[:, o_z:])
    cc = cc_ref[...]
    ss = ss_ref[...]
    kv = _rms(kv_lat) * gkv_ref[...]
    kpe = _rope_slab(pe2, cc, ss)
    if decode:
        rows_ref[:, :A_KV_LORA] = kv
        rows_ref[:, A_KV_LORA:] = kpe[:, :A_ROPE]
    else:
        rows_ref[:A_KV_LORA, :] = kv.T
        rows_ref[A_KV_LORA:, :] = kpe.T[:A_ROPE, :]
    qn = (_rms(q_lat) * gq_ref[...]).astype(BF16)
    q_scale = A_SCALE if decode else A_SCALE * LOG2_E
    if not decode:
        kvb = kv.astype(BF16)
        kpe_b = kpe.astype(BF16)
    for hd in range(A_HEADS):
        lo = hd * A_HEAD_PAD
        qh = _dot(qn, wuq_ref[:, lo:lo + A_HEAD_PAD])
        nope = qh[:, :A_NOPE]
        qpe = (_rope_slab(qh[:, A_NOPE:], cc, ss) * q_scale).astype(BF16)
        if decode:
            qabs = _dot(nope.astype(BF16), wa_ref[hd])
            qabs_ref[:, hd * A_KV_LORA:(hd + 1) * A_KV_LORA] = (qabs * A_SCALE).astype(BF16)
            qpe_ref[:, hd * LANE:(hd + 1) * LANE] = qpe
        else:
            qcat_ref[:, lo:lo + A_NOPE] = (nope * q_scale).astype(BF16)
            qcat_ref[:, lo + A_NOPE:lo + A_HEAD_PAD] = qpe
            knope = _dot(kvb, wa_ref[:, hd * A_NOPE:(hd + 1) * A_NOPE])
            kcat_ref[:, lo:lo + A_NOPE] = knope.astype(BF16)
            kcat_ref[:, lo + A_NOPE:lo + A_HEAD_PAD] = kpe_b
            wv_t = wb_ref[hd * A_VDIM:(hd + 1) * A_VDIM, :]
            v_ref[hd] = _dot_nt(wv_t, kvb).astype(BF16)


def _mla_proj(decode, x, shift, scale, cc, ss, win, gq, gkv, wuq, wa, wb, tm, tiles_per_seq):
    m = x.shape[0]
    mod_spec = _mod_specs(decode, tm, tiles_per_seq)
    tab_spec = pl.BlockSpec((tm, LANE), lambda i: (i % tiles_per_seq, 0))
    row = lambda n: pl.BlockSpec((tm, n), lambda i: (i, 0))
    if decode:
        rows_shape, rows_spec = (m, A_CACHE_W), row(A_CACHE_W)
    else:
        rows_shape = (m // (tm * tiles_per_seq), A_CACHE_W, tm * tiles_per_seq)
        rows_spec = pl.BlockSpec((None, A_CACHE_W, tm),
                                 lambda i: (i // tiles_per_seq, 0, i % tiles_per_seq))
    out_shape = [jax.ShapeDtypeStruct((m, A_WIDTH), F32),
                 jax.ShapeDtypeStruct(rows_shape, F32)]
    out_specs = [row(A_WIDTH), rows_spec]
    if decode:
        out_shape += [jax.ShapeDtypeStruct((m, A_HEADS * A_KV_LORA), BF16),
                      jax.ShapeDtypeStruct((m, A_HEADS * LANE), BF16)]
        out_specs += [row(A_HEADS * A_KV_LORA), row(A_HEADS * LANE)]
    else:
        out_shape += [jax.ShapeDtypeStruct((m, A_HEADS * A_HEAD_PAD), BF16),
                      jax.ShapeDtypeStruct((m, A_HEADS * A_HEAD_PAD), BF16),
                      jax.ShapeDtypeStruct(
                          (BATCH, A_HEADS, SEQ // FLASH_TK, A_VDIM, FLASH_TK), BF16)]
        per_blk = FLASH_TK // tm

        def v_map(i):
            t = i % tiles_per_seq
            return (i // tiles_per_seq, 0, t // per_blk, 0, t % per_blk)

        out_specs += [row(A_HEADS * A_HEAD_PAD), row(A_HEADS * A_HEAD_PAD),
                      pl.BlockSpec((None, A_HEADS, None, A_VDIM, tm), v_map)]
    return pl.pallas_call(
        functools.partial(_mla_proj_kernel, decode),
        out_shape=out_shape,
        grid=(m // tm,),
        in_specs=[row(D_MODEL), mod_spec, mod_spec, tab_spec, tab_spec,
                  _resident(win.shape), _resident(gq.shape), _resident(gkv.shape),
                  _resident(wuq.shape), _resident(wa.shape), _resident(wb.shape)],
        out_specs=out_specs,
        compiler_params=_params(1),
        name="mla_proj_decode" if decode else "mla_proj_prompt",
    )(x, shift, scale, cc, ss, win, gq, gkv, wuq, wa, wb)


FLASH_TQ = 1024
FLASH_TK = 512
FLASH_CHAIN = 256
FLASH_DEN_ROWS = 16


def _flash_kernel(q_ref, k_ref, vt_ref, o_ref, m_ref, acc_ref, sa_ref, sb_ref):
    tq, tk, cw = FLASH_TQ, FLASH_TK, FLASH_CHAIN
    n_chain = tq // cw
    qi = pl.program_id(2)
    m_ref[...] = jnp.full(m_ref.shape, NEG_INF, F32)
    acc_ref[...] = jnp.zeros(acc_ref.shape, F32)
    ones = jnp.ones((FLASH_DEN_ROWS, tk), BF16)

    def update(c, s, vt_blk):
        m_prev = m_ref[c]
        m_new = jnp.maximum(m_prev, jnp.max(s, 0, keepdims=True))
        alpha = jnp.exp2(m_prev - m_new)
        p = jnp.exp2(s - m_new).astype(BF16)
        acc_ref[c] = alpha * acc_ref[c] + _dot(vt_blk, p)
        m_ref[c] = m_new

    def visible(key_off):
        return [c for c in range(n_chain) if key_off is None or key_off <= (c + 1) * cw - 1]

    def score(ki, s_ref, key_off):
        start = pl.multiple_of(ki * tk, tk)
        k_blk = k_ref[pl.ds(start, tk), :]
        for c in visible(key_off):
            s_ref[c] = _dot_nt(k_blk, q_ref[c * cw:(c + 1) * cw, :])

    def consume(ki, s_ref, key_off):
        vt_blk = jnp.concatenate([vt_ref[ki], ones], 0)
        for c in visible(key_off):
            s = s_ref[c]
            if key_off is not None and key_off + tk - 1 > c * cw:
                key = lax.broadcasted_iota(jnp.int32, (tk, cw), 0) + key_off
                qry = lax.broadcasted_iota(jnp.int32, (tk, cw), 1) + c * cw
                s = jnp.where(key <= qry, s, NEG_INF)
            update(c, s, vt_blk)

    assert tq == 2 * tk
    score(0, sa_ref, None)

    def body(j, carry):
        score(2 * j + 1, sb_ref, None)
        consume(2 * j, sa_ref, None)
        score(2 * j + 2, sa_ref, None)
        consume(2 * j + 1, sb_ref, None)
        return carry

    lax.fori_loop(0, qi, body, 0)
    score(2 * qi + 1, sb_ref, tk)
    consume(2 * qi, sa_ref, 0)
    consume(2 * qi + 1, sb_ref, tk)
    for c in range(n_chain):
        den = acc_ref[c, A_VDIM:A_VDIM + 1, :]
        o_t = acc_ref[c, :A_VDIM, :] * (1.0 / den)
        o_ref[c * cw:(c + 1) * cw, :] = o_t.T.astype(o_ref.dtype)


def _flash(qcat, kcat, vt):
    tq, tk, cw = FLASH_TQ, FLASH_TK, FLASH_CHAIN
    nq, nk, n_chain = SEQ // tq, SEQ // tk, tq // cw
    return pl.pallas_call(
        _flash_kernel,
        out_shape=jax.ShapeDtypeStruct((BATCH * SEQ, A_WIDTH), BF16),
        grid=(BATCH, A_HEADS, nq),
        in_specs=[
            pl.BlockSpec((tq, A_HEAD_PAD), lambda b, h, i: (b * nq + i, h)),
            pl.BlockSpec((SEQ, A_HEAD_PAD), lambda b, h, i: (b, h)),
            pl.BlockSpec((None, None, nk, A_VDIM, tk), lambda b, h, i: (b, h, 0, 0, 0)),
        ],
        out_specs=pl.BlockSpec((tq, A_VDIM), lambda b, h, i: (b * nq + i, h)),
        scratch_shapes=[pltpu.VMEM((n_chain, 1, cw), F32),
                        pltpu.VMEM((n_chain, A_VDIM + FLASH_DEN_ROWS, cw), F32),
                        pltpu.VMEM((n_chain, tk, cw), F32), pltpu.VMEM((n_chain, tk, cw), F32)],
        compiler_params=_params(3),
        name="mla_flash",
    )(qcat, kcat, vt)


DEC_PAGES_PER_STEP = 32
DEC_GROUPS = 4


def _mla_decode_kernel(pt_ref, qabs_ref, qpe_ref, row_ref, cache_ref, o_ref,
                       m_ref, l_ref, acc_ref, buf_ref, sem_ref):
    n = DEC_PAGES_PER_STEP
    b = pl.program_id(0)
    j = pl.program_id(1)
    n_chunk = pl.num_programs(1)
    step = b * n_chunk + j
    slot = lax.rem(step, 2)

    def page_copy(seq, chunk, i, half):
        page = pt_ref[seq, chunk * n + i]
        return pltpu.make_async_copy(cache_ref.at[0, page], buf_ref.at[half, i], sem_ref.at[half])

    @pl.when(step == 0)
    def _():
        for i in range(n):
            page_copy(0, 0, i, 0).start()

    wraps = j + 1 == n_chunk
    next_seq = jnp.where(wraps, b + 1, b)
    next_chunk = jnp.where(wraps, 0, j + 1)

    @pl.when(next_seq < pl.num_programs(0))
    def _():
        for i in range(n):
            page_copy(next_seq, next_chunk, i, 1 - slot).start()

    for i in range(n):
        page_copy(b, j, i, slot).wait()
    page_refs = [buf_ref.at[slot, i] for i in range(n)]

    @pl.when(j == 0)
    def _():
        m_ref[...] = jnp.full(m_ref.shape, NEG_INF, F32)
        l_ref[...] = jnp.zeros(l_ref.shape, F32)
        acc_ref[...] = jnp.zeros(acc_ref.shape, F32)

    qa = qabs_ref[...]
    qp = qpe_ref[:, :A_ROPE]
    per_group = n // DEC_GROUPS

    def score(g):
        refs = page_refs[g * per_group:(g + 1) * per_group]
        pages = [r[...].astype(BF16) for r in refs]
        kv_t = jnp.concatenate([pg[:A_KV_LORA] for pg in pages], 1)
        pe_t = jnp.concatenate([pg[A_KV_LORA:] for pg in pages], 1)
        return _dot(qa, kv_t) + _dot(qp, pe_t), kv_t

    def update(s, kv_t):
        m_prev = m_ref[...]
        m_new = jnp.maximum(m_prev, jnp.max(s, -1, keepdims=True))
        alpha = jnp.exp(m_prev - m_new)
        p = jnp.exp(s - m_new)
        l_ref[...] = alpha * l_ref[...] + jnp.sum(p, -1, keepdims=True)
        acc_ref[...] = alpha * acc_ref[...] + _dot_nt(p.astype(BF16), kv_t)
        m_ref[...] = m_new

    pending = score(0)
    for g in range(1, DEC_GROUPS):
        ahead = score(g)
        update(*pending)
        pending = ahead
    update(*pending)

    @pl.when(j == pl.num_programs(1) - 1)
    def _():
        rb = row_ref[...].astype(BF16).astype(F32)
        s_new = (jnp.sum(qa.astype(F32) * rb[:, :A_KV_LORA], -1, keepdims=True)
                 + jnp.sum(qp.astype(F32) * rb[:, A_KV_LORA:], -1, keepdims=True))
        m_old = m_ref[...]
        m_fin = jnp.maximum(m_old, s_new)
        corr = jnp.exp(m_old - m_fin)
        p_new = jnp.exp(s_new - m_fin)
        l_fin = l_ref[...] * corr + p_new
        acc_fin = acc_ref[...] * corr + p_new.astype(BF16).astype(F32) * rb[:, :A_KV_LORA]
        o_ref[...] = acc_fin / l_fin


def _mla_decode(page_table, qabs, qpe, rows, cache):
    n = DEC_PAGES_PER_STEP
    grid_spec = pltpu.PrefetchScalarGridSpec(
        num_scalar_prefetch=1,
        grid=(DEC_BATCH, N_PAGES // n),
        in_specs=[
            pl.BlockSpec((None, A_HEADS, A_KV_LORA), lambda b, j, pt: (b, 0, 0)),
            pl.BlockSpec((None, A_HEADS, LANE), lambda b, j, pt: (b, 0, 0)),
            pl.BlockSpec((None, 1, A_CACHE_W), lambda b, j, pt: (b, 0, 0)),
            pl.BlockSpec(memory_space=pl.ANY),
        ],
        out_specs=pl.BlockSpec((None, A_HEADS, A_KV_LORA), lambda b, j, pt: (b, 0, 0)),
        scratch_shapes=[pltpu.VMEM((A_HEADS, 1), F32), pltpu.VMEM((A_HEADS, 1), F32),
                        pltpu.VMEM((A_HEADS, A_KV_LORA), F32),
                        pltpu.VMEM((2, n, A_CACHE_W, PAGE_SIZE), F32),
                        pltpu.SemaphoreType.DMA((2,))],
    )
    return pl.pallas_call(
        _mla_decode_kernel,
        out_shape=jax.ShapeDtypeStruct((DEC_BATCH, A_HEADS, A_KV_LORA), F32),
        grid_spec=grid_spec,
        compiler_params=_params(2),
        name="mla_decode",
    )(page_table, qabs, qpe, rows, cache)


def _uv_kernel(o_ref, w_ref, out_ref):
    out_ref[...] = _dot(o_ref[...].astype(BF16), w_ref[...]).astype(out_ref.dtype)


def _mla_decode_uv(o_lat2d, wuv_heads):
    m = o_lat2d.shape[0]
    return pl.pallas_call(
        _uv_kernel,
        out_shape=jax.ShapeDtypeStruct((m, A_WIDTH), BF16),
        grid=(A_HEADS,),
        in_specs=[pl.BlockSpec((m, A_KV_LORA), lambda h: (0, h)),
                  pl.BlockSpec((None, A_KV_LORA, A_VDIM), lambda h: (h, 0, 0))],
        out_specs=pl.BlockSpec((m, A_VDIM), lambda h: (0, h)),
        compiler_params=_params(1),
        name="mla_decode_uv",
    )(o_lat2d, wuv_heads)


def _outproj_kernel(final, o_ref, z_ref, w_ref, b_ref, x_ref, gate_ref, *rest):
    g = (o_ref[...].astype(F32) * _silu(z_ref[...])).astype(BF16)
    y = _dot(g, w_ref[...]) + b_ref[...]
    x_new = x_ref[...] + gate_ref[...] * y
    if final:
        gfin_ref, out_ref = rest
        out_ref[...] = _rms(x_new) * gfin_ref[...]
    else:
        (out_ref,) = rest
        out_ref[...] = x_new


def _outproj(final, o, z, w, b, x, gate, gfin, per_token, tm, tiles_per_seq):
    m = x.shape[0]
    row = lambda n: pl.BlockSpec((tm, n), lambda i: (i, 0))
    in_specs = [row(o.shape[1]), row(z.shape[1]), _resident(w.shape), _resident(b.shape),
                row(D_MODEL), _mod_specs(per_token, tm, tiles_per_seq)]
    args = [o, z, w, b, x, gate]
    if final:
        in_specs.append(_resident(gfin.shape))
        args.append(gfin)
    return pl.pallas_call(
        functools.partial(_outproj_kernel, final),
        out_shape=jax.ShapeDtypeStruct((m, D_MODEL), F32),
        grid=(m // tm,),
        in_specs=in_specs,
        out_specs=row(D_MODEL),
        compiler_params=_params(1),
        name="outproj_final" if final else "outproj",
    )(*args)


def _swa_proj_kernel(q_scale, x_ref, shift_ref, scale_ref, w_ref, b_ref, q_ref, k_ref, v_ref, z_ref):
    h = (_rms(x_ref[...]) * (1.0 + scale_ref[...]) + shift_ref[...]).astype(BF16)
    o_k, o_v, o_z = B_WIDTH, B_WIDTH + B_KV_W, B_WIDTH + 2 * B_KV_W
    q = _dot(h, w_ref[:, :o_k]) + b_ref[:, :o_k]
    q_ref[...] = (q * q_scale).astype(BF16)
    k_ref[...] = _dot(h, w_ref[:, o_k:o_v]) + b_ref[:, o_k:o_v]
    v_ref[...] = _dot(h, w_ref[:, o_v:o_z]) + b_ref[:, o_v:o_z]
    z_ref[...] = _dot(h, w_ref[:, o_z:]) + b_ref[:, o_z:]


def _swa_proj(q_scale, x, shift, scale, w, b, per_token, tm, tiles_per_seq):
    m = x.shape[0]
    row = lambda n: pl.BlockSpec((tm, n), lambda i: (i, 0))
    mod_spec = _mod_specs(per_token, tm, tiles_per_seq)
    return pl.pallas_call(
        functools.partial(_swa_proj_kernel, q_scale),
        out_shape=[jax.ShapeDtypeStruct((m, B_WIDTH), BF16),
                   jax.ShapeDtypeStruct((m, B_KV_W), F32),
                   jax.ShapeDtypeStruct((m, B_KV_W), F32),
                   jax.ShapeDtypeStruct((m, B_WIDTH), F32)],
        grid=(m // tm,),
        in_specs=[row(D_MODEL), mod_spec, mod_spec, _resident(w.shape), _resident(b.shape)],
        out_specs=[row(B_WIDTH), row(B_KV_W), row(B_KV_W), row(B_WIDTH)],
        compiler_params=_params(1),
        name="swa_proj",
    )(x, shift, scale, w, b)


def _block_diag2(t):
    zero = jnp.zeros_like(t)
    return jnp.concatenate([jnp.concatenate([t, zero], 1), jnp.concatenate([zero, t], 1)], 0)


def _swa_prompt_kernel(sink_ref, q_ref, kc_ref, kp_ref, vc_ref, vp_ref, bias_ref, o_ref):
    w = WINDOW
    kband = jnp.concatenate([kp_ref[...], kc_ref[...]], 0).astype(BF16)
    vband = jnp.concatenate([vp_ref[...], vc_ref[...]], 0).astype(BF16)
    pair_w = 2 * B_HEAD_DIM
    n_pair = B_HEADS // 2
    pairs_per_kv = B_GROUP // 2
    one = jnp.ones((2 * w, B_HEAD_DIM), BF16)
    ones_bd = _block_diag2(one)
    first_head = lax.broadcasted_iota(jnp.int32, (w, pair_w), 1) < B_HEAD_DIM
    scores, vbs = [], []
    for kvh in range(B_KV_HEADS):
        kb = _block_diag2(kband[:, kvh * B_HEAD_DIM:(kvh + 1) * B_HEAD_DIM])
        vb = _block_diag2(vband[:, kvh * B_HEAD_DIM:(kvh + 1) * B_HEAD_DIM])
        vbs.append(jnp.concatenate([vb, ones_bd], 1))
        for gp in range(pairs_per_kv):
            lo = (kvh * pairs_per_kv + gp) * pair_w
            scores.append(_dot_nt(q_ref[:, lo:lo + pair_w], kb))
    probs, sink_terms = [], []
    for idx in range(n_pair):
        both, sink_p = [], []
        for t in range(2):
            head = 2 * idx + t
            s = scores[idx][:, t * 2 * w:(t + 1) * 2 * w] + bias_ref[head]
            sk = sink_ref[head]
            m = jnp.maximum(jnp.max(s, -1, keepdims=True), sk)
            both.append(jnp.exp2(s - m).astype(BF16))
            sink_p.append(jnp.exp2(sk - m))
        probs.append(jnp.concatenate(both, 1))
        sink_terms.append(jnp.where(first_head, sink_p[0], sink_p[1]))
    for idx in range(n_pair):
        o_den = _dot(probs[idx], vbs[idx // pairs_per_kv])
        den = o_den[:, pair_w:] + sink_terms[idx]
        o_ref[:, idx * pair_w:(idx + 1) * pair_w] = (
            o_den[:, :pair_w] * (1.0 / den)).astype(o_ref.dtype)


def _swa_prompt_bias():
    i = jnp.arange(WINDOW)[:, None]
    j = jnp.arange(2 * WINDOW)[None, :]
    dist = i - j + WINDOW
    valid = (dist >= 0) & (dist < WINDOW)
    slopes = jnp.power(2.0, -8.0 * jnp.arange(1, B_HEADS + 1, dtype=F32) / B_HEADS)
    alibi = -slopes[:, None, None] * dist.astype(F32)
    alibi = alibi * LOG2_E
    later = jnp.where(valid[None], alibi, NEG_INF)
    first = jnp.where((valid & (j >= WINDOW))[None], alibi, NEG_INF)
    return jnp.stack([first, later])


def _swa_prompt(sinks, q, k, v):
    nb = SEQ // WINDOW
    cur = lambda b, n, s: (b * nb + n, 0)
    prev = lambda b, n, s: (b * nb + jnp.maximum(n - 1, 0), 0)
    grid_spec = pltpu.PrefetchScalarGridSpec(
        num_scalar_prefetch=1,
        grid=(BATCH, nb),
        in_specs=[pl.BlockSpec((WINDOW, B_WIDTH), cur),
                  pl.BlockSpec((WINDOW, B_KV_W), cur), pl.BlockSpec((WINDOW, B_KV_W), prev),
                  pl.BlockSpec((WINDOW, B_KV_W), cur), pl.BlockSpec((WINDOW, B_KV_W), prev),
                  pl.BlockSpec((None, B_HEADS, WINDOW, 2 * WINDOW),
                               lambda b, n, s: (jnp.minimum(n, 1), 0, 0, 0))],
        out_specs=pl.BlockSpec((WINDOW, B_WIDTH), cur),
    )
    return pl.pallas_call(
        _swa_prompt_kernel,
        out_shape=jax.ShapeDtypeStruct((BATCH * SEQ, B_WIDTH), BF16),
        grid_spec=grid_spec,
        compiler_params=_params(2),
        name="swa_prompt",
    )(sinks, q, k, k, v, v, _swa_prompt_bias())


SWA_DEC_TILE = 16


def _swa_decode_kernel(bias_ref, sink_ref, q_ref, knew_ref, vnew_ref, kbuf_ref, vbuf_ref,
                       o_ref, kout_ref, vout_ref):
    w = WINDOW
    last = lax.broadcasted_iota(jnp.int32, (B_KV_W, w), 1) == w - 1
    diag = (lax.broadcasted_iota(jnp.int32, (B_KV_W, B_KV_W), 0)
            == lax.broadcasted_iota(jnp.int32, (B_KV_W, B_KV_W), 1))

    def column(row):
        return jnp.sum(jnp.where(diag, row, 0.0), -1, keepdims=True)

    group_bits = B_GROUP.bit_length() - 1
    dim_bits = B_HEAD_DIM.bit_length() - 1
    own = (jnp.right_shift(lax.broadcasted_iota(jnp.int32, (B_HEADS, B_KV_W), 0), group_bits)
           == jnp.right_shift(lax.broadcasted_iota(jnp.int32, (B_HEADS, B_KV_W), 1), dim_bits))
    scores, vals = [], []
    for i in range(SWA_DEC_TILE):
        kb = jnp.where(last, column(knew_ref[i:i + 1, :]), pltpu.roll(kbuf_ref[i], w - 1, 1))
        vb = jnp.where(last, column(vnew_ref[i:i + 1, :]), pltpu.roll(vbuf_ref[i], w - 1, 1))
        kout_ref[i] = kb
        vout_ref[i] = vb
        q_wide = jnp.concatenate([q_ref[i]] * B_KV_HEADS, 1)
        q_bd = jnp.where(own, q_wide, 0.0).astype(BF16)
        scores.append(_dot(q_bd, kb.astype(BF16)))
        vals.append(vb.astype(BF16))
    s = jnp.concatenate(scores, 0) + bias_ref[...]
    sk = sink_ref[...]
    m = jnp.maximum(jnp.max(s, -1, keepdims=True), sk)
    p = jnp.exp(s - m)
    den = jnp.sum(p, -1, keepdims=True) + jnp.exp(sk - m)
    pn = (p * (1.0 / den)).astype(BF16)
    kv_of_row = jnp.right_shift(
        lax.broadcasted_iota(jnp.int32, (B_HEADS, B_HEAD_DIM), 0), group_bits)
    for i in range(SWA_DEC_TILE):
        o_all = _dot_nt(pn[i * B_HEADS:(i + 1) * B_HEADS], vals[i])
        o = jnp.zeros((B_HEADS, B_HEAD_DIM), F32)
        for kvh in range(B_KV_HEADS):
            part = o_all[:, kvh * B_HEAD_DIM:(kvh + 1) * B_HEAD_DIM]
            o = o + jnp.where(kv_of_row == kvh, part, 0.0)
        o_ref[i] = o


def _swa_decode(bias, sinks, q, knew, vnew, kbuf, vbuf):
    t = SWA_DEC_TILE
    head_tab = pl.BlockSpec((t * B_HEADS, LANE), lambda i: (0, 0))
    qo = pl.BlockSpec((t, B_HEADS, B_HEAD_DIM), lambda i: (i, 0, 0))
    new = pl.BlockSpec((t, B_KV_W), lambda i: (i, 0))
    buf = pl.BlockSpec((t, B_KV_W, WINDOW), lambda i: (i, 0, 0))
    return pl.pallas_call(
        _swa_decode_kernel,
        out_shape=[jax.ShapeDtypeStruct((DEC_BATCH, B_HEADS, B_HEAD_DIM), F32),
                   jax.ShapeDtypeStruct(kbuf.shape, F32),
                   jax.ShapeDtypeStruct(vbuf.shape, F32)],
        grid=(DEC_BATCH // t,),
        in_specs=[head_tab, head_tab, qo, new, new, buf, buf],
        out_specs=[qo, buf, buf],
        compiler_params=_params(1),
        name="swa_decode",
    )(bias, sinks, q, knew, vnew, kbuf, vbuf)


def _rope_tables(pos):
    inv = jnp.power(ROPE_THETA, -jnp.arange(0, A_ROPE, 2, dtype=F32) / A_ROPE)
    ang = pos.astype(F32)[:, None] * inv[None, :]
    zero = jnp.zeros((pos.shape[0], LANE - A_ROPE), F32)
    cos, sin = jnp.cos(ang), jnp.sin(ang)
    return jnp.concatenate([cos, cos, zero], -1), jnp.concatenate([sin, sin, zero], -1)


def _with_rot(w):
    half = A_ROPE // 2
    return jnp.concatenate([w, -w[..., half:], w[..., :half]], -1)


def kernel(x_prompt, x_sample, cache_mla, state_swa_k, state_swa_v, page_table, c_prompt, c_sample,
           w_ada, b_ada, wa_in, ga_q, ga_kv, wa_uq, wa_uk, wa_uv, wa_o,
           wb_in, bb_in, wb_sinks, wb_o, bb_o, g_final):
    n_p = BATCH * SEQ
    xp = x_prompt.reshape(n_p, D_MODEL)
    xs = x_sample.reshape(DEC_BATCH, D_MODEL)

    pad = jnp.zeros((8 - BATCH, D_MODEL), F32)
    mod = _ada_mod(jnp.concatenate([c_sample, c_prompt, pad], 0), w_ada, b_ada)

    def mods(i):
        parts = jnp.split(mod[i], 3, -1)
        sample = [p[:DEC_BATCH] for p in parts]
        prompt = [p[DEC_BATCH:DEC_BATCH + BATCH].reshape(BATCH, 1, D_MODEL) for p in parts]
        return prompt, sample

    (shift_p, scale_p, gate_p), (shift_s, scale_s, gate_s) = mods(0)
    o_pe = A_Q_LORA + A_KV_LORA
    w_in = wa_in[0].astype(BF16)
    win = jnp.concatenate([w_in[:, :o_pe], _with_rot(w_in[:, o_pe:o_pe + A_ROPE]),
                           w_in[:, o_pe + A_ROPE:]], -1)
    wuq3 = wa_uq[0].astype(BF16).reshape(A_Q_LORA, A_HEADS, A_QK)
    wuq = jnp.concatenate([wuq3[..., :A_NOPE], _with_rot(wuq3[..., A_NOPE:])], -1).reshape(
        A_Q_LORA, A_HEADS * A_HEAD_PAD)
    wuk_b = wa_uk[0].astype(BF16)
    wuv_b = wa_uv[0].astype(BF16)
    wuk = wuk_b.reshape(A_KV_LORA, A_HEADS * A_NOPE)
    wuv_t = wuv_b.reshape(A_KV_LORA, A_WIDTH).T
    wuk_t = wuk_b.transpose(1, 2, 0)
    wuv_h = wuv_b.transpose(1, 0, 2)
    gq = ga_q[0].reshape(1, A_Q_LORA)
    gkv = ga_kv[0].reshape(1, A_KV_LORA)
    wo_a = wa_o[0].astype(BF16)
    zero_bias = jnp.zeros((1, D_MODEL), F32)

    tm_p = 256
    tiles_p = SEQ // tm_p
    tm_w = 512
    tiles_w = SEQ // tm_w
    cc_p, ss_p = _rope_tables(jnp.arange(SEQ))
    z_p, rows_p, qcat, kcat, v_p = _mla_proj(
        False, xp, shift_p, scale_p, cc_p, ss_p, win, gq, gkv, wuq, wuk, wuv_t, tm_p, tiles_p)
    o_p = _flash(qcat, kcat, v_p)
    xp = _outproj(False, o_p, z_p, wo_a, zero_bias, xp, gate_p, None, False, tm_w, tiles_w)

    cc_s, ss_s = _rope_tables(jnp.full((DEC_BATCH,), PAST_LEN))
    z_s, rows_s, qabs, qpe = _mla_proj(
        True, xs, shift_s, scale_s, cc_s, ss_s, win, gq, gkv, wuq, wuk_t, wuv_t, DEC_BATCH, 1)
    o_lat = _mla_decode(page_table,
                        qabs.reshape(DEC_BATCH, A_HEADS, A_KV_LORA),
                        qpe.reshape(DEC_BATCH, A_HEADS, LANE),
                        rows_s.reshape(DEC_BATCH, 1, A_CACHE_W), jnp.swapaxes(cache_mla, 2, 3))
    o_s = _mla_decode_uv(o_lat.reshape(DEC_BATCH, A_HEADS * A_KV_LORA), wuv_h)
    xs = _outproj(False, o_s, z_s, wo_a, zero_bias, xs, gate_s, None, True, DEC_BATCH, 1)

    (shift_p, scale_p, gate_p), (shift_s, scale_s, gate_s) = mods(1)
    w_b = wb_in[0].astype(BF16)
    b_b = bb_in[0].reshape(1, B_IN)
    wo_b = wb_o[0].astype(BF16)
    bo_b = bb_o[0].reshape(1, D_MODEL)
    gfin = g_final.reshape(1, D_MODEL)
    sinks = wb_sinks[0]

    q_p, k_p, v_p, z_p = _swa_proj(B_SCALE * LOG2_E, xp, shift_p, scale_p, w_b, b_b, False,
                                   tm_w, tiles_w)
    o_p = _swa_prompt(sinks * LOG2_E, q_p, k_p, v_p)
    y_p = _outproj(True, o_p, z_p, wo_b, bo_b, xp, gate_p, gfin, False, tm_w, tiles_w)

    q_s, k_s, v_s, z_s = _swa_proj(B_SCALE, xs, shift_s, scale_s, w_b, b_b, True, DEC_BATCH, 1)
    slopes = jnp.power(2.0, -8.0 * jnp.arange(1, B_HEADS + 1, dtype=F32) / B_HEADS)
    behind = (WINDOW - 1 - jnp.arange(WINDOW)).astype(F32)
    bias_s = jnp.tile(-slopes[:, None] * behind[None, :], (SWA_DEC_TILE, 1))
    sink_s = jnp.tile(jnp.broadcast_to(sinks[:, None], (B_HEADS, LANE)), (SWA_DEC_TILE, 1))

    def channel_major(t):
        return jnp.transpose(t, (0, 2, 3, 1)).reshape(DEC_BATCH, B_KV_W, WINDOW)

    o_s, kwin, vwin = _swa_decode(
        bias_s, sink_s, q_s.astype(F32).reshape(DEC_BATCH, B_HEADS, B_HEAD_DIM), k_s, v_s,
        channel_major(state_swa_k[0]), channel_major(state_swa_v[0]))
    y_s = _outproj(True, o_s.reshape(DEC_BATCH, B_WIDTH), z_s, wo_b, bo_b, xs, gate_s, gfin,
                   True, DEC_BATCH, 1)

    kv_shape = (1, BATCH, WINDOW, B_KV_HEADS, B_HEAD_DIM)

    def window_major(t):
        t = t.reshape(DEC_BATCH, B_KV_HEADS, B_HEAD_DIM, WINDOW)
        return jnp.transpose(t, (0, 3, 1, 2))[None]

    return (y_p.reshape(BATCH, SEQ, D_MODEL),
            y_s.reshape(DEC_BATCH, 1, D_MODEL),
            jnp.swapaxes(rows_p, 1, 2)[None],
            rows_s.reshape(1, DEC_BATCH, 1, A_CACHE_W),
            k_p.reshape(BATCH, SEQ, B_KV_W)[:, SEQ - WINDOW:].reshape(kv_shape),
            v_p.reshape(BATCH, SEQ, B_KV_W)[:, SEQ - WINDOW:].reshape(kv_shape),
            window_major(kwin),
            window_major(vwin))
```
